```python
import math
import jax, jax.numpy as jnp
from jax import lax
import numpy as np

D_MODEL = 2048
BATCH = 1
SEQ = 8192
DEPTH = 2
DEC_BATCH = 4
DEC_SEQ = 4096
PAST_LEN = 128

CONV_WIDTH = 1024
CONV_KERNEL = 31
CONV_PAD = CONV_KERNEL // 2
N_HEADS = 8
HEAD_DIM = 64
V_DIM = 2 * HEAD_DIM
QK_WIDTH = N_HEADS * 2 * HEAD_DIM
ATTN_WIDTH = N_HEADS * V_DIM
MIX_WIDTH = CONV_WIDTH + ATTN_WIDTH
IN_WIDTH = 2 * CONV_WIDTH + 2 * QK_WIDTH + ATTN_WIDTH
Q_BLOCK = 128
N_BUCKETS = 32
MAX_DISTANCE = 128
N_EXPERTS = 16
CAPACITY_FACTOR = 2
D_FF = 5632
EPS = 1e-6

kernel_name = "hybrid_conv_diffattn_ec_moe_encoder"


def rms_norm(x, g):
    xf = x.astype(jnp.float32)
    y = xf * lax.rsqrt(jnp.mean(xf * xf, axis=-1, keepdims=True) + EPS)
    return (y * g.astype(jnp.float32)).astype(x.dtype)


def layer_norm(x, g, b):
    xf = x.astype(jnp.float32)
    mu = jnp.mean(xf, axis=-1, keepdims=True)
    var = jnp.mean(jnp.square(xf - mu), axis=-1, keepdims=True)
    y = (xf - mu) * lax.rsqrt(var + EPS)
    return (y * g.astype(jnp.float32) + b.astype(jnp.float32)).astype(x.dtype)


def rel_bucket(rel):
    half = N_BUCKETS // 2
    max_exact = half // 2
    ret = (rel > 0).astype(jnp.int32) * half
    n = jnp.abs(rel)
    nf = jnp.maximum(n, 1).astype(jnp.float32)
    large = max_exact + (jnp.log(nf / max_exact) / math.log(MAX_DISTANCE / max_exact)
                         * (half - max_exact)).astype(jnp.int32)
    large = jnp.minimum(large, half - 1)
    return ret + jnp.where(n < max_exact, n, large)


def diff_attention(q, k, v, lam, rel_bias):
    B, S = q.shape[0], q.shape[1]
    nblk = S // Q_BLOCK
    scale = HEAD_DIM ** -0.5
    k1, k2 = k[..., 0, :], k[..., 1, :]
    qb = q.reshape(B, nblk, Q_BLOCK, N_HEADS, 2, HEAD_DIM).transpose(1, 0, 2, 3, 4, 5)
    kpos = jnp.arange(S, dtype=jnp.int32)
    starts = jnp.arange(nblk, dtype=jnp.int32) * Q_BLOCK

    def block(args):
        qblk, start = args
        qpos = start + jnp.arange(Q_BLOCK, dtype=jnp.int32)
        buckets = rel_bucket(kpos[None, :] - qpos[:, None])
        bias = rel_bias[buckets].transpose(2, 0, 1).astype(jnp.float32)
        s1 = jnp.einsum('bqhd,bkhd->bhqk', qblk[..., 0, :], k1).astype(jnp.float32) * scale + bias
        s2 = jnp.einsum('bqhd,bkhd->bhqk', qblk[..., 1, :], k2).astype(jnp.float32) * scale + bias
        a = jax.nn.softmax(s1, axis=-1) - lam * jax.nn.softmax(s2, axis=-1)
        return jnp.einsum('bhqk,bkhe->bqhe', a.astype(v.dtype), v)

    o = lax.map(block, (qb, starts))
    return o.transpose(1, 0, 2, 3, 4).reshape(B, S, N_HEADS, V_DIM)


def expert_choice_ffn(x, w_router, w_gate, w_up, w_down):
    n = x.shape[0]
    cap = CAPACITY_FACTOR * n // N_EXPERTS
    logits = jnp.einsum('nd,de->ne', x, w_router).astype(jnp.float32)
    aff = jax.nn.softmax(logits, axis=-1)
    gates, idx = lax.top_k(aff.T, cap)
    xe = x[idx]
    h = jax.nn.silu(jnp.einsum('ecd,edf->ecf', xe, w_gate)) * jnp.einsum('ecd,edf->ecf', xe, w_up)
    ye = jnp.einsum('ecf,efd->ecd', h, w_down) * gates[..., None].astype(x.dtype)
    return jnp.zeros_like(x).at[idx.reshape(-1)].add(ye.reshape(-1, x.shape[-1]))


def encoder_layer(x, l, rel_bias, ln1_g, w_in, conv_w, conv_b, conv_ln_g, conv_ln_b,
                  lam_q1, lam_k1, lam_q2, lam_k2, subln_g, w_out, ln2_g,
                  w_router, w_gate, w_up, w_down):
    B, S, D = x.shape
    h = rms_norm(x, ln1_g[l])
    z = jnp.einsum('bsd,dc->bsc', h, w_in[l])
    cv, cg, q, k, v = jnp.split(
        z, [CONV_WIDTH, 2 * CONV_WIDTH, 2 * CONV_WIDTH + QK_WIDTH, 2 * CONV_WIDTH + 2 * QK_WIDTH], axis=-1)

    u = cv * jax.nn.sigmoid(cg)
    u = lax.conv_general_dilated(
        u, conv_w[l][:, None, :].astype(u.dtype), window_strides=(1,),
        padding=[(CONV_PAD, CONV_PAD)], dimension_numbers=('NWC', 'WIO', 'NWC'),
        feature_group_count=CONV_WIDTH) + conv_b[l]
    u = jax.nn.silu(layer_norm(u, conv_ln_g[l], conv_ln_b[l]))

    lam_init = 0.8 - 0.6 * math.exp(-0.3 * l)
    lam = (jnp.exp(jnp.sum(lam_q1[l].astype(jnp.float32) * lam_k1[l].astype(jnp.float32)))
           - jnp.exp(jnp.sum(lam_q2[l].astype(jnp.float32) * lam_k2[l].astype(jnp.float32)))
           + lam_init)
    o = diff_attention(q.reshape(B, S, N_HEADS, 2, HEAD_DIM), k.reshape(B, S, N_HEADS, 2, HEAD_DIM),
                       v.reshape(B, S, N_HEADS, V_DIM), lam, rel_bias)
    o = rms_norm(o, subln_g[l]) * (1.0 - lam_init)

    mix = jnp.concatenate([u, o.reshape(B, S, ATTN_WIDTH)], axis=-1)
    x = x + jnp.einsum('bsc,cd->bsd', mix, w_out[l])

    h = rms_norm(x, ln2_g[l]).reshape(B * S, D)
    y = expert_choice_ffn(h, w_router[l], w_gate[l], w_up[l], w_down[l])
    return x + y.reshape(B, S, D)


def setup_inputs(seed: int = 0) -> dict:
    key = jax.random.key(seed)
    ks = jax.random.split(key, 24)
    f32 = jnp.float32
    nrm = lambda k, shape, s: jax.random.normal(k, shape, f32) * s
    return {
        "x_prompt": nrm(ks[0], (BATCH, SEQ, D_MODEL), 1.0),
        "x_sample": nrm(ks[1], (DEC_BATCH, DEC_SEQ, D_MODEL), 1.0),
        "rel_bias": nrm(ks[2], (N_BUCKETS, N_HEADS), 0.5),
        "final_g": 1.0 + nrm(ks[3], (D_MODEL,), 0.02),
        "ln1_g": 1.0 + nrm(ks[4], (DEPTH, D_MODEL), 0.02),
        "w_in": nrm(ks[5], (DEPTH, D_MODEL, IN_WIDTH), D_MODEL ** -0.5),
        "conv_w": nrm(ks[6], (DEPTH, CONV_KERNEL, CONV_WIDTH), CONV_KERNEL ** -0.5),
        "conv_b": nrm(ks[7], (DEPTH, CONV_WIDTH), 0.02),
        "conv_ln_g": 1.0 + nrm(ks[8], (DEPTH, CONV_WIDTH), 0.02),
        "conv_ln_b": nrm(ks[9], (DEPTH, CONV_WIDTH), 0.02),
        "lam_q1": nrm(ks[10], (DEPTH, HEAD_DIM), 0.1),
        "lam_k1": nrm(ks[11], (DEPTH, HEAD_DIM), 0.1),
        "lam_q2": nrm(ks[12], (DEPTH, HEAD_DIM), 0.1),
        "lam_k2": nrm(ks[13], (DEPTH, HEAD_DIM), 0.1),
        "subln_g": 1.0 + nrm(ks[14], (DEPTH, V_DIM), 0.02),
        "w_out": nrm(ks[15], (DEPTH, MIX_WIDTH, D_MODEL), MIX_WIDTH ** -0.5),
        "ln2_g": 1.0 + nrm(ks[16], (DEPTH, D_MODEL), 0.02),
        "w_router": nrm(ks[17], (DEPTH, D_MODEL, N_EXPERTS), D_MODEL ** -0.5),
        "w_gate": nrm(ks[18], (DEPTH, N_EXPERTS, D_MODEL, D_FF), D_MODEL ** -0.5),
        "w_up": nrm(ks[19], (DEPTH, N_EXPERTS, D_MODEL, D_FF), D_MODEL ** -0.5),
        "w_down": nrm(ks[20], (DEPTH, N_EXPERTS, D_FF, D_MODEL), D_FF ** -0.5),
    }


def reference(x_prompt, x_sample, rel_bias, final_g, ln1_g, w_in, conv_w, conv_b, conv_ln_g,
              conv_ln_b, lam_q1, lam_k1, lam_q2, lam_k2, subln_g, w_out, ln2_g,
              w_router, w_gate, w_up, w_down):
    xp, xs = x_prompt, x_sample
    for l in range(DEPTH):
        xp = encoder_layer(xp, l, rel_bias, ln1_g, w_in, conv_w, conv_b, conv_ln_g, conv_ln_b,
                           lam_q1, lam_k1, lam_q2, lam_k2, subln_g, w_out, ln2_g,
                           w_router, w_gate, w_up, w_down)
        xs = encoder_layer(xs, l, rel_bias, ln1_g, w_in, conv_w, conv_b, conv_ln_g, conv_ln_b,
                           lam_q1, lam_k1, lam_q2, lam_k2, subln_g, w_out, ln2_g,
                           w_router, w_gate, w_up, w_down)
    y_prompt = rms_norm(xp, final_g)
    y_sample = rms_norm(xs, final_g)
    return (y_prompt, y_sample)
```

```python
import functools
import math

import jax
import jax.numpy as jnp
from jax import lax
from jax.experimental import pallas as pl
from jax.experimental.pallas import tpu as pltpu

EPS = 1e-6
CONV_KERNEL = 31
CONV_PAD = CONV_KERNEL // 2
HALO = 16
HEAD_DIM = 64
V_DIM = 2 * HEAD_DIM
N_BUCKETS = 32
MAX_DISTANCE = 128
CAPACITY_FACTOR = 2
LANES = 128
VMEM_LIMIT_BYTES = 56 * 1024 * 1024

_NT = (((1,), (1,)), ((), ()))


def _tile(n, pref):
    t = min(n, pref)
    while n % t:
        t -= 1
    return t


def _params(*sem):
    return pltpu.CompilerParams(dimension_semantics=sem, vmem_limit_bytes=VMEM_LIMIT_BYTES)


def _norm_matmul_kernel(x_ref, g_ref, w_ref, o_ref, h_ref):
    @pl.when(pl.program_id(1) == 0)
    def _():
        x = x_ref[...]
        ms = jnp.mean(x * x, axis=-1, keepdims=True)
        h_ref[...] = (x * lax.rsqrt(ms + EPS) * g_ref[...]).astype(h_ref.dtype)

    o_ref[...] = jnp.dot(h_ref[...], w_ref[...], preferred_element_type=jnp.float32).astype(o_ref.dtype)


def _norm_matmul(x, g, w, out_dtype):
    t, d = x.shape
    n = w.shape[1]
    tm, tn = _tile(t, 512), _tile(n, 1024)
    return pl.pallas_call(
        _norm_matmul_kernel,
        grid=(t // tm, n // tn),
        in_specs=[
            pl.BlockSpec((tm, d), lambda i, j: (i, 0)),
            pl.BlockSpec((1, d), lambda i, j: (0, 0)),
            pl.BlockSpec((d, tn), lambda i, j: (0, j)),
        ],
        out_specs=pl.BlockSpec((tm, tn), lambda i, j: (i, j)),
        out_shape=jax.ShapeDtypeStruct((t, n), out_dtype),
        scratch_shapes=[pltpu.VMEM((tm, d), jnp.bfloat16)],
        compiler_params=_params("parallel", "arbitrary"),
        name="norm_matmul",
    )(x, g.reshape(1, d), w)


def _conv_kernel(prev_ref, cur_ref, next_ref, w_ref, b_ref, lg_ref, lb_ref, o_ref, e_ref, c_ref,
                 *, ts, cw, n_p, s_p, s_s):
    r0 = pl.program_id(0) * ts
    in_p = r0 < n_p
    local = jnp.where(in_p, r0 % s_p, (r0 - n_p) % s_s)
    slen = jnp.where(in_p, s_p, s_s)
    keep_prev = (local != 0).astype(jnp.float32)
    keep_next = (local + ts != slen).astype(jnp.float32)

    def glu(z):
        return z[:, :cw] * jax.nn.sigmoid(z[:, cw:])

    e_ref[0:HALO, :] = glu(prev_ref[...]) * keep_prev
    e_ref[HALO:HALO + ts, :] = glu(cur_ref[...])
    e_ref[HALO + ts:, :] = glu(next_ref[...]) * keep_next

    off = HALO - CONV_PAD
    for c in range(cw // LANES):
        cs = slice(c * LANES, (c + 1) * LANES)
        acc = jnp.broadcast_to(b_ref[:, cs], (ts, LANES))
        for j in range(CONV_KERNEL):
            acc = acc + w_ref[j:j + 1, cs] * e_ref[off + j:off + j + ts, cs]
        c_ref[:, cs] = acc

    u = c_ref[...]
    mu = jnp.mean(u, axis=-1, keepdims=True)
    var = jnp.mean(jnp.square(u - mu), axis=-1, keepdims=True)
    y = (u - mu) * lax.rsqrt(var + EPS) * lg_ref[...] + lb_ref[...]
    o_ref[...] = (y * jax.nn.sigmoid(y)).astype(o_ref.dtype)


def _conv_module(zc, conv_w, conv_b, ln_g, ln_b, n_p, s_p, s_s):
    t = zc.shape[0]
    cw = zc.shape[1] // 2
    ts = _tile(math.gcd(s_p, s_s), 256)
    hb = ts // HALO
    last_hb = t // HALO - 1
    kern = functools.partial(_conv_kernel, ts=ts, cw=cw, n_p=n_p, s_p=s_p, s_s=s_s)
    return pl.pallas_call(
        kern,
        grid=(t // ts,),
        in_specs=[
            pl.BlockSpec((HALO, 2 * cw), lambda i: (jnp.maximum(i * hb - 1, 0), 0)),
            pl.BlockSpec((ts, 2 * cw), lambda i: (i, 0)),
            pl.BlockSpec((HALO, 2 * cw), lambda i: (jnp.minimum((i + 1) * hb, last_hb), 0)),
            pl.BlockSpec((CONV_KERNEL, cw), lambda i: (0, 0)),
            pl.BlockSpec((1, cw), lambda i: (0, 0)),
            pl.BlockSpec((1, cw), lambda i: (0, 0)),
            pl.BlockSpec((1, cw), lambda i: (0, 0)),
        ],
        out_specs=pl.BlockSpec((ts, cw), lambda i: (i, 0)),
        out_shape=jax.ShapeDtypeStruct((t, cw), jnp.bfloat16),
        scratch_shapes=[pltpu.VMEM((ts + 2 * HALO, cw), jnp.float32), pltpu.VMEM((ts, cw), jnp.float32)],
        compiler_params=_params("parallel"),
        name="conv_module",
    )(zc, zc, zc, conv_w, conv_b.reshape(1, cw), ln_g.reshape(1, cw), ln_b.reshape(1, cw))


def _rel_bucket(rel):
    half = N_BUCKETS // 2
    max_exact = half // 2
    ret = (rel > 0).astype(jnp.int32) * half
    n = jnp.abs(rel)
    nf = jnp.maximum(n, 1).astype(jnp.float32)
    large = max_exact + (jnp.log(nf / max_exact) / math.log(MAX_DISTANCE / max_exact)
                         * (half - max_exact)).astype(jnp.int32)
    large = jnp.minimum(large, half - 1)
    return ret + jnp.where(n < max_exact, n, large)


def _bias_tiles(rel_bias, t):
    d = jnp.arange(-2, 3, dtype=jnp.int32)[:, None, None]
    a = jnp.arange(t, dtype=jnp.int32)[None, :, None]
    b = jnp.arange(t, dtype=jnp.int32)[None, None, :]
    buckets = _rel_bucket(d * t + b - a)
    return rel_bias[buckets].transpose(0, 3, 1, 2).astype(jnp.float32)


def _attn_kernel(lam_ref, q_ref, k_ref, v_ref, bias_ref, g_ref, o_ref,
                 m1_ref, l1_ref, a1_ref, m2_ref, l2_ref, a2_ref, *, post_scale):
    j = pl.program_id(2)

    @pl.when(j == 0)
    def _():
        for m_ref, l_ref, a_ref in ((m1_ref, l1_ref, a1_ref), (m2_ref, l2_ref, a2_ref)):
            m_ref[...] = jnp.full(m_ref.shape, -jnp.inf, jnp.float32)
            l_ref[...] = jnp.zeros(l_ref.shape, jnp.float32)
            a_ref[...] = jnp.zeros(a_ref.shape, jnp.float32)

    q = q_ref[...] * (HEAD_DIM ** -0.5)
    lane = lax.broadcasted_iota(jnp.int32, q.shape, 1)
    zero = jnp.zeros_like(q)
    k = k_ref[...]
    v = v_ref[...]
    bias = bias_ref[0, 0]

    def update(qm, m_ref, l_ref, a_ref):
        s = lax.dot_general(qm, k, _NT, preferred_element_type=jnp.float32) + bias
        m_prev = m_ref[...]
        m_new = jnp.maximum(m_prev, jnp.max(s, axis=-1, keepdims=True))
        alpha = jnp.exp(m_prev - m_new)
        p = jnp.exp(s - m_new)
        l_ref[...] = alpha * l_ref[...] + jnp.sum(p, axis=-1, keepdims=True)
        a_ref[...] = alpha * a_ref[...] + jnp.dot(p.astype(v.dtype), v, preferred_element_type=jnp.float32)
        m_ref[...] = m_new

    update(jnp.where(lane < HEAD_DIM, q, zero), m1_ref, l1_ref, a1_ref)
    update(jnp.where(lane >= HEAD_DIM, q, zero), m2_ref, l2_ref, a2_ref)

    @pl.when(j == pl.num_programs(2) - 1)
    def _():
        o = a1_ref[...] / l1_ref[...] - lam_ref[0] * (a2_ref[...] / l2_ref[...])
        ms = jnp.mean(o * o, axis=-1, keepdims=True)
        o = o * lax.rsqrt(ms + EPS) * g_ref[...] * post_scale
        o_ref[...] = o.astype(o_ref.dtype)


def _diff_attention(zqkv, row0, batch, seq, n_heads, lam, bias_tiles, subln_g, post_scale):
    t = bias_tiles.shape[-1]
    nt = seq // t
    rb0 = row0 // t
    kern = functools.partial(_attn_kernel, post_scale=post_scale)

    def row(bh, i):
        return rb0 + (bh // n_heads) * nt + i

    return pl.pallas_call(
        kern,
        grid=(batch * n_heads, nt, nt),
        in_specs=[
            pl.BlockSpec(memory_space=pltpu.SMEM),
            pl.BlockSpec((t, V_DIM), lambda bh, i, j: (row(bh, i), bh % n_heads)),
            pl.BlockSpec((t, V_DIM), lambda bh, i, j: (row(bh, j), n_heads + bh % n_heads)),
            pl.BlockSpec((t, V_DIM), lambda bh, i, j: (row(bh, j), 2 * n_heads + bh % n_heads)),
            pl.BlockSpec((1, 1, t, t), lambda bh, i, j: (jnp.clip(j - i, -2, 2) + 2, bh % n_heads, 0, 0)),
            pl.BlockSpec((1, V_DIM), lambda bh, i, j: (0, 0)),
        ],
        out_specs=pl.BlockSpec((t, V_DIM), lambda bh, i, j: ((bh // n_heads) * nt + i, bh % n_heads)),
        out_shape=jax.ShapeDtypeStruct((batch * seq, n_heads * V_DIM), jnp.bfloat16),
        scratch_shapes=[pltpu.VMEM((t, 1), jnp.float32), pltpu.VMEM((t, 1), jnp.float32),
                        pltpu.VMEM((t, V_DIM), jnp.float32)] * 2,
        compiler_params=_params("parallel", "parallel", "arbitrary"),
        name="diff_attention",
    )(lam.reshape(1), zqkv, zqkv, zqkv, bias_tiles, subln_g.reshape(1, V_DIM))


def _outproj_kernel(x_ref, u_ref, a_ref, wu_ref, wa_ref, g_ref, wr_ref, x1_ref, h_ref, aff_ref, *, n_experts):
    y = jnp.dot(u_ref[...], wu_ref[...], preferred_element_type=jnp.float32)
    y = y + jnp.dot(a_ref[...], wa_ref[...], preferred_element_type=jnp.float32)
    x1 = x_ref[...] + y
    x1_ref[...] = x1
    ms = jnp.mean(x1 * x1, axis=-1, keepdims=True)
    h = (x1 * lax.rsqrt(ms + EPS) * g_ref[...]).astype(h_ref.dtype)
    h_ref[...] = h
    logits = jnp.dot(h, wr_ref[...], preferred_element_type=jnp.float32)
    lane = lax.broadcasted_iota(jnp.int32, logits.shape, 1)
    logits = jnp.where(lane < n_experts, logits, -jnp.inf)
    e = jnp.exp(logits - jnp.max(logits, axis=-1, keepdims=True))
    aff_ref[...] = e / jnp.sum(e, axis=-1, keepdims=True)


def _outproj_router(x, u, a, w_u, w_a, g, w_r, n_experts):
    t, d = x.shape
    cw, aw = u.shape[1], a.shape[1]
    tm = _tile(t, 256)
    kern = functools.partial(_outproj_kernel, n_experts=n_experts)
    return pl.pallas_call(
        kern,
        grid=(t // tm,),
        in_specs=[
            pl.BlockSpec((tm, d), lambda i: (i, 0)),
            pl.BlockSpec((tm, cw), lambda i: (i, 0)),
            pl.BlockSpec((tm, aw), lambda i: (i, 0)),
            pl.BlockSpec((cw, d), lambda i: (0, 0)),
            pl.BlockSpec((aw, d), lambda i: (0, 0)),
            pl.BlockSpec((1, d), lambda i: (0, 0)),
            pl.BlockSpec((d, LANES), lambda i: (0, 0)),
        ],
        out_specs=[
            pl.BlockSpec((tm, d), lambda i: (i, 0)),
            pl.BlockSpec((tm, d), lambda i: (i, 0)),
            pl.BlockSpec((tm, LANES), lambda i: (i, 0)),
        ],
        out_shape=[
            jax.ShapeDtypeStruct((t, d), jnp.float32),
            jax.ShapeDtypeStruct((t, d), jnp.bfloat16),
            jax.ShapeDtypeStruct((t, LANES), jnp.float32),
        ],
        compiler_params=_params("parallel"),
        name="outproj_router",
    )(x, u, a, w_u, w_a, g.reshape(1, d), w_r)


def _gate_up_kernel(x_ref, wg_ref, wu_ref, o_ref):
    x = x_ref[0]
    g = jnp.dot(x, wg_ref[0].astype(x.dtype), preferred_element_type=jnp.float32)
    u = jnp.dot(x, wu_ref[0].astype(x.dtype), preferred_element_type=jnp.float32)
    o_ref[0] = (g * jax.nn.sigmoid(g) * u).astype(o_ref.dtype)


def _expert_gate_up(xe, w_gate, w_up):
    e, m, d = xe.shape
    f = w_gate.shape[2]
    tm, tn = _tile(m, 1024), _tile(f, 512)
    return pl.pallas_call(
        _gate_up_kernel,
        grid=(e, f // tn, m // tm),
        in_specs=[
            pl.BlockSpec((1, tm, d), lambda ei, n, mi: (ei, mi, 0)),
            pl.BlockSpec((1, d, tn), lambda ei, n, mi: (ei, 0, n)),
            pl.BlockSpec((1, d, tn), lambda ei, n, mi: (ei, 0, n)),
        ],
        out_specs=pl.BlockSpec((1, tm, tn), lambda ei, n, mi: (ei, mi, n)),
        out_shape=jax.ShapeDtypeStruct((e, m, f), jnp.bfloat16),
        compiler_params=_params("parallel", "parallel", "arbitrary"),
        name="expert_gate_up",
    )(xe, w_gate, w_up)


def _down_kernel(h_ref, w_ref, gate_ref, o_ref):
    h = h_ref[0]
    y = jnp.dot(h, w_ref[0].astype(h.dtype), preferred_element_type=jnp.float32)
    reps = y.shape[1] // LANES
    o_ref[0] = y * jnp.concatenate([gate_ref[0]] * reps, axis=1)


def _expert_down(h, w_down, gates_b):
    e, m, f = h.shape
    d = w_down.shape[2]
    tm, tn = _tile(m, 1024), _tile(d, 256)
    return pl.pallas_call(
        _down_kernel,
        grid=(e, m // tm, d // tn),
        in_specs=[
            pl.BlockSpec((1, tm, f), lambda ei, mi, n: (ei, mi, 0)),
            pl.BlockSpec((1, f, tn), lambda ei, mi, n: (ei, 0, n)),
            pl.BlockSpec((1, tm, LANES), lambda ei, mi, n: (ei, mi, 0)),
        ],
        out_specs=pl.BlockSpec((1, tm, tn), lambda ei, mi, n: (ei, mi, n)),
        out_shape=jax.ShapeDtypeStruct((e, m, d), jnp.float32),
        compiler_params=_params("parallel", "parallel", "arbitrary"),
        name="expert_down",
    )(h, w_down, gates_b)


def _rmsnorm_kernel(x_ref, g_ref, o_ref):
    x = x_ref[...]
    ms = jnp.mean(x * x, axis=-1, keepdims=True)
    o_ref[...] = x * lax.rsqrt(ms + EPS) * g_ref[...]


def _rmsnorm(x, g):
    t, d = x.shape
    tm = _tile(t, 512)
    return pl.pallas_call(
        _rmsnorm_kernel,
        grid=(t // tm,),
        in_specs=[pl.BlockSpec((tm, d), lambda i: (i, 0)), pl.BlockSpec((1, d), lambda i: (0, 0))],
        out_specs=pl.BlockSpec((tm, d), lambda i: (i, 0)),
        out_shape=jax.ShapeDtypeStruct((t, d), jnp.float32),
        compiler_params=_params("parallel"),
        name="final_rmsnorm",
    )(x, g.reshape(1, d))


def _route(aff, h, n_experts):
    n = aff.shape[0]
    cap = CAPACITY_FACTOR * n // n_experts
    gates, idx = lax.top_k(aff[:, :n_experts].T, cap)
    return gates, idx, h[idx]


def kernel(x_prompt, x_sample, rel_bias, final_g, ln1_g, w_in, conv_w, conv_b, conv_ln_g, conv_ln_b,
           lam_q1, lam_k1, lam_q2, lam_k2, subln_g, w_out, ln2_g, w_router, w_gate, w_up, w_down):
    bp, sp, d = x_prompt.shape
    bs, ss, _ = x_sample.shape
    n_p, n_s = bp * sp, bs * ss
    depth = w_in.shape[0]
    cw = conv_w.shape[2]
    n_heads = (w_in.shape[2] - 2 * cw) // (3 * V_DIM)
    n_experts = w_router.shape[2]
    bf16 = jnp.bfloat16

    x = jnp.concatenate([x_prompt.reshape(n_p, d), x_sample.reshape(n_s, d)], axis=0)
    t_attn = _tile(math.gcd(sp, ss), 512)
    bias_tiles = _bias_tiles(rel_bias, t_attn)

    for l in range(depth):
        w_in_l = w_in[l].astype(bf16)
        zc = _norm_matmul(x, ln1_g[l], w_in_l[:, :2 * cw], jnp.float32)
        zqkv = _norm_matmul(x, ln1_g[l], w_in_l[:, 2 * cw:], bf16)

        u = _conv_module(zc, conv_w[l], conv_b[l], conv_ln_g[l], conv_ln_b[l], n_p, sp, ss)

        lam_init = 0.8 - 0.6 * math.exp(-0.3 * l)
        lam = (jnp.exp(jnp.sum(lam_q1[l] * lam_k1[l])) - jnp.exp(jnp.sum(lam_q2[l] * lam_k2[l])) + lam_init)
        attn = functools.partial(_diff_attention, zqkv, n_heads=n_heads, lam=lam, bias_tiles=bias_tiles,
                                 subln_g=subln_g[l], post_scale=1.0 - lam_init)
        a = jnp.concatenate([attn(row0=0, batch=bp, seq=sp), attn(row0=n_p, batch=bs, seq=ss)], axis=0)

        w_out_l = w_out[l].astype(bf16)
        w_r = jnp.pad(w_router[l], ((0, 0), (0, LANES - n_experts))).astype(bf16)
        x1, h, aff = _outproj_router(x, u, a, w_out_l[:cw], w_out_l[cw:], ln2_g[l], w_r, n_experts)

        gates_p, idx_p, xe_p = _route(aff[:n_p], h[:n_p], n_experts)
        gates_s, idx_s, xe_s = _route(aff[n_p:], h[n_p:], n_experts)
        xe = jnp.concatenate([xe_p, xe_s], axis=1)
        gates = jnp.concatenate([gates_p, gates_s], axis=1)
        idx = jnp.concatenate([idx_p, idx_s + n_p], axis=1)

        hid = _expert_gate_up(xe, w_gate[l], w_up[l])
        gates_b = jnp.broadcast_to(gates[..., None], gates.shape + (LANES,))
        ye = _expert_down(hid, w_down[l], gates_b)
        x = x1.at[idx.reshape(-1)].add(ye.reshape(-1, d))

    y = _rmsnorm(x, final_g)
    return y[:n_p].reshape(bp, sp, d), y[n_p:].reshape(bs, ss, d)
```

```python
import functools
import math

import jax
import jax.numpy as jnp
from jax import lax
from jax.experimental import pallas as pl
from jax.experimental.pallas import tpu as pltpu

EPS = 1e-6
CONV_KERNEL = 31
CONV_PAD = CONV_KERNEL // 2
HALO = 16
HEAD_DIM = 64
V_DIM = 2 * HEAD_DIM
N_BUCKETS = 32
MAX_DISTANCE = 128
CAPACITY_FACTOR = 2
LANES = 128
SUBLANES = 8
LOG2E = math.log2(math.e)
VMEM_LIMIT_BYTES = 56 * 1024 * 1024

_NT = (((1,), (1,)), ((), ()))


def _tile(n, pref):
    t = min(n, pref)
    while n % t:
        t -= 1
    return t


def _params(*sem):
    return pltpu.CompilerParams(dimension_semantics=sem, vmem_limit_bytes=VMEM_LIMIT_BYTES)


def _norm_matmul_kernel(x_ref, g_ref, w_ref, s_ref, o_ref, h_ref):
    @pl.when(pl.program_id(1) == 0)
    def _():
        x = x_ref[...]
        ms = jnp.mean(x * x, axis=-1, keepdims=True)
        h_ref[...] = (x * lax.rsqrt(ms + EPS) * g_ref[...]).astype(h_ref.dtype)

    y = jnp.dot(h_ref[...], w_ref[...], preferred_element_type=jnp.float32)
    o_ref[...] = (y * s_ref[...]).astype(o_ref.dtype)


def _norm_matmul(x, g, w, col_scale, out_dtype):
    t, d = x.shape
    n = w.shape[1]
    tm, tn = _tile(t, 512), _tile(n, 1024)
    return pl.pallas_call(
        _norm_matmul_kernel,
        grid=(t // tm, n // tn),
        in_specs=[
            pl.BlockSpec((tm, d), lambda i, j: (i, 0)),
            pl.BlockSpec((1, d), lambda i, j: (0, 0)),
            pl.BlockSpec((d, tn), lambda i, j: (0, j)),
            pl.BlockSpec((1, tn), lambda i, j: (0, j)),
        ],
        out_specs=pl.BlockSpec((tm, tn), lambda i, j: (i, j)),
        out_shape=jax.ShapeDtypeStruct((t, n), out_dtype),
        scratch_shapes=[pltpu.VMEM((tm, d), jnp.bfloat16)],
        compiler_params=_params("parallel", "arbitrary"),
        name="norm_matmul",
    )(x, g.reshape(1, d), w, col_scale.reshape(1, n))


def _conv_kernel(prev_ref, cur_ref, next_ref, w_ref, b_ref, lg_ref, lb_ref, o_ref, e_ref, c_ref,
                 *, ts, cw, n_p, s_p, s_s):
    r0 = pl.program_id(0) * ts
    in_p = r0 < n_p
    local = jnp.where(in_p, r0 % s_p, (r0 - n_p) % s_s)
    slen = jnp.where(in_p, s_p, s_s)
    keep_prev = (local != 0).astype(jnp.float32)
    keep_next = (local + ts != slen).astype(jnp.float32)

    def glu(z):
        return z[:, :cw] * jax.nn.sigmoid(z[:, cw:])

    e_ref[0:HALO, :] = glu(prev_ref[...]) * keep_prev
    e_ref[HALO:HALO + ts, :] = glu(cur_ref[...])
    e_ref[HALO + ts:, :] = glu(next_ref[...]) * keep_next

    off = HALO - CONV_PAD
    for c in range(cw // LANES):
        cs = slice(c * LANES, (c + 1) * LANES)
        acc = jnp.broadcast_to(b_ref[:, cs], (ts, LANES))
        for j in range(CONV_KERNEL):
            acc = acc + w_ref[j:j + 1, cs] * e_ref[off + j:off + j + ts, cs]
        c_ref[:, cs] = acc

    u = c_ref[...]
    mu = jnp.mean(u, axis=-1, keepdims=True)
    var = jnp.mean(jnp.square(u - mu), axis=-1, keepdims=True)
    y = (u - mu) * lax.rsqrt(var + EPS) * lg_ref[...] + lb_ref[...]
    o_ref[...] = (y * jax.nn.sigmoid(y)).astype(o_ref.dtype)


def _conv_module(zc, conv_w, conv_b, ln_g, ln_b, n_p, s_p, s_s):
    t = zc.shape[0]
    cw = zc.shape[1] // 2
    ts = _tile(math.gcd(s_p, s_s), 256)
    hb = ts // HALO
    last_hb = t // HALO - 1
    kern = functools.partial(_conv_kernel, ts=ts, cw=cw, n_p=n_p, s_p=s_p, s_s=s_s)
    return pl.pallas_call(
        kern,
        grid=(t // ts,),
        in_specs=[
            pl.BlockSpec((HALO, 2 * cw), lambda i: (jnp.maximum(i * hb - 1, 0), 0)),
            pl.BlockSpec((ts, 2 * cw), lambda i: (i, 0)),
            pl.BlockSpec((HALO, 2 * cw), lambda i: (jnp.minimum((i + 1) * hb, last_hb), 0)),
            pl.BlockSpec((CONV_KERNEL, cw), lambda i: (0, 0)),
            pl.BlockSpec((1, cw), lambda i: (0, 0)),
            pl.BlockSpec((1, cw), lambda i: (0, 0)),
            pl.BlockSpec((1, cw), lambda i: (0, 0)),
        ],
        out_specs=pl.BlockSpec((ts, cw), lambda i: (i, 0)),
        out_shape=jax.ShapeDtypeStruct((t, cw), jnp.bfloat16),
        scratch_shapes=[pltpu.VMEM((ts + 2 * HALO, cw), jnp.float32), pltpu.VMEM((ts, cw), jnp.float32)],
        compiler_params=_params("parallel"),
        name="conv_module",
    )(zc, zc, zc, conv_w, conv_b.reshape(1, cw), ln_g.reshape(1, cw), ln_b.reshape(1, cw))


def _rel_bucket(rel):
    half = N_BUCKETS // 2
    max_exact = half // 2
    ret = (rel > 0).astype(jnp.int32) * half
    n = jnp.abs(rel)
    nf = jnp.maximum(n, 1).astype(jnp.float32)
    large = max_exact + (jnp.log(nf / max_exact) / math.log(MAX_DISTANCE / max_exact)
                         * (half - max_exact)).astype(jnp.int32)
    large = jnp.minimum(large, half - 1)
    return ret + jnp.where(n < max_exact, n, large)


def _bias_tiles(rel_bias, tq, tk):
    assert tk > MAX_DISTANCE and tq % tk == 0
    d = jnp.arange(-2, tq // tk + 2, dtype=jnp.int32)[:, None, None]
    kk = jnp.arange(tk, dtype=jnp.int32)[None, :, None]
    qq = jnp.arange(tq, dtype=jnp.int32)[None, None, :]
    buckets = _rel_bucket(d * tk + kk - qq)[None]
    table = rel_bias.astype(jnp.float32).T * LOG2E
    out = jnp.zeros((table.shape[0],) + buckets.shape[1:], jnp.float32)
    for bkt in range(N_BUCKETS):
        out = jnp.where(buckets == bkt, table[:, bkt][:, None, None, None], out)
    return out


def _attn_kernel(lam_ref, q_ref, k_ref, vt_ref, bias_ref, g_ref, o_ref, qm_ref, st_ref, sc_ref,
                 *, tq, tk, post_scale):
    i = pl.program_id(1)
    nkv = k_ref.shape[0] // tk
    strips = tq // tk
    q = q_ref[...]
    lane = lax.broadcasted_iota(jnp.int32, q.shape, 1)
    zero = jnp.zeros_like(q)
    qm_ref[0] = jnp.where(lane < HEAD_DIM, q, zero)
    qm_ref[1] = jnp.where(lane >= HEAD_DIM, q, zero)
    r_m, r_l = V_DIM, V_DIM + SUBLANES
    st_ref[:, :r_m, :] = jnp.zeros((2, r_m, tq), jnp.float32)
    st_ref[:, r_m:r_l, :] = jnp.full((2, SUBLANES, tq), -jnp.inf, jnp.float32)
    st_ref[:, r_l:, :] = jnp.zeros((2, SUBLANES, tq), jnp.float32)

    chains = [(mp, slice(s_i * tk, (s_i + 1) * tk)) for mp in range(2) for s_i in range(strips)]

    def scores(c):
        k = k_ref[pl.ds(pl.multiple_of(c * tk, tk), tk), :]
        return jnp.stack([lax.dot_general(k, qm_ref[mp, qs, :], _NT, preferred_element_type=jnp.float32)
                          for mp, qs in chains])

    sc_ref[0] = scores(0)

    def body(c, _):
        nxt = scores(jnp.minimum(c + 1, nkv - 1))
        cur = sc_ref[c % 2]
        vt = vt_ref[0, c]
        tile = jnp.clip(c - i * strips, -2, strips + 1) + 2
        st = st_ref[...]
        stats, probs = [], []
        for n, (mp, qs) in enumerate(chains):
            s = cur[n] + bias_ref[0, tile, :, qs]
            m_prev = st[mp, r_m:r_l, qs]
            m_new = jnp.maximum(m_prev, jnp.max(s, axis=0, keepdims=True))
            alpha = jnp.exp2(m_prev - m_new)
            p = jnp.exp2(s - m_new[:1])
            l_new = alpha * st[mp, r_l:, qs] + jnp.sum(p, axis=0, keepdims=True)
            stats.append((alpha, m_new, l_new))
            probs.append(p.astype(vt.dtype))
        blocks = []
        for (mp, qs), p, (alpha, m_new, l_new) in zip(chains, probs, stats):
            a_new = alpha[:1] * st[mp, :r_m, qs] + jnp.dot(vt, p, preferred_element_type=jnp.float32)
            blocks.append(jnp.concatenate([a_new, m_new, l_new], axis=0))
        st_ref[...] = jnp.stack([jnp.concatenate(blocks[mp * strips:(mp + 1) * strips], axis=1)
                                 for mp in range(2)])
        sc_ref[(c + 1) % 2] = nxt
        return 0

    lax.fori_loop(0, nkv, body, 0)

    l1, l2 = st_ref[0, r_l:r_l + 1, :], st_ref[1, r_l:r_l + 1, :]
    o = st_ref[0, :r_m, :] / l1 - lam_ref[0] * (st_ref[1, :r_m, :] / l2)
    ms = jnp.mean(o * o, axis=0, keepdims=True)
    o = (o * lax.rsqrt(ms + EPS)).T * (g_ref[...] * post_scale)
    o_ref[...] = o.astype(o_ref.dtype)


def _diff_attention(zqkv, vt, row0, batch, seq, n_heads, lam, bias_tiles, subln_g, post_scale):
    _, n_tiles, tk, tq = bias_tiles.shape
    nt = seq // tq
    rb0 = row0 // tq
    kern = functools.partial(_attn_kernel, tq=tq, tk=tk, post_scale=post_scale)
    return pl.pallas_call(
        kern,
        grid=(batch * n_heads, nt),
        in_specs=[
            pl.BlockSpec(memory_space=pltpu.SMEM),
            pl.BlockSpec((tq, V_DIM), lambda bh, i: (rb0 + (bh // n_heads) * nt + i, bh % n_heads)),
            pl.BlockSpec((seq, V_DIM), lambda bh, i: (row0 // seq + bh // n_heads, n_heads + bh % n_heads)),
            pl.BlockSpec((1, seq // tk, V_DIM, tk),
                         lambda bh, i: (bh % n_heads, row0 // seq + bh // n_heads, 0, 0)),
            pl.BlockSpec((1, n_tiles, tk, tq), lambda bh, i: (bh % n_heads, 0, 0, 0)),
            pl.BlockSpec((1, V_DIM), lambda bh, i: (0, 0)),
        ],
        out_specs=pl.BlockSpec((tq, V_DIM), lambda bh, i: ((bh // n_heads) * nt + i, bh % n_heads)),
        out_shape=jax.ShapeDtypeStruct((batch * seq, n_heads * V_DIM), jnp.bfloat16),
        scratch_shapes=[pltpu.VMEM((2, tq, V_DIM), jnp.bfloat16),
                        pltpu.VMEM((2, V_DIM + 2 * SUBLANES, tq), jnp.float32),
                        pltpu.VMEM((2, 2 * tq // tk, tk, tk), jnp.float32)],
        compiler_params=_params("parallel", "arbitrary"),
        name="diff_attention",
    )(lam.reshape(1), zqkv, zqkv, vt, bias_tiles, subln_g.reshape(1, V_DIM))


def _outproj_kernel(x_ref, u_ref, a_ref, wu_ref, wa_ref, g_ref, wr_ref, x1_ref, h_ref, aff_ref, *, n_experts):
    y = jnp.dot(u_ref[...], wu_ref[...], preferred_element_type=jnp.float32)
    y = y + jnp.dot(a_ref[...], wa_ref[...], preferred_element_type=jnp.float32)
    x1 = x_ref[...] + y
    x1_ref[...] = x1
    ms = jnp.mean(x1 * x1, axis=-1, keepdims=True)
    h = (x1 * lax.rsqrt(ms + EPS) * g_ref[...]).astype(h_ref.dtype)
    h_ref[...] = h
    logits = jnp.dot(h, wr_ref[...], preferred_element_type=jnp.float32)
    lane = lax.broadcasted_iota(jnp.int32, logits.shape, 1)
    logits = jnp.where(lane < n_experts, logits, -jnp.inf)
    e = jnp.exp(logits - jnp.max(logits, axis=-1, keepdims=True))
    aff_ref[...] = e / jnp.sum(e, axis=-1, keepdims=True)


def _outproj_router(x, u, a, w_u, w_a, g, w_r, n_experts):
    t, d = x.shape
    cw, aw = u.shape[1], a.shape[1]
    tm = _tile(t, 256)
    kern = functools.partial(_outproj_kernel, n_experts=n_experts)
    return pl.pallas_call(
        kern,
        grid=(t // tm,),
        in_specs=[
            pl.BlockSpec((tm, d), lambda i: (i, 0)),
            pl.BlockSpec((tm, cw), lambda i: (i, 0)),
            pl.BlockSpec((tm, aw), lambda i: (i, 0)),
            pl.BlockSpec((cw, d), lambda i: (0, 0)),
            pl.BlockSpec((aw, d), lambda i: (0, 0)),
            pl.BlockSpec((1, d), lambda i: (0, 0)),
            pl.BlockSpec((d, LANES), lambda i: (0, 0)),
        ],
        out_specs=[
            pl.BlockSpec((tm, d), lambda i: (i, 0)),
            pl.BlockSpec((tm, d), lambda i: (i, 0)),
            pl.BlockSpec((tm, LANES), lambda i: (i, 0)),
        ],
        out_shape=[
            jax.ShapeDtypeStruct((t, d), jnp.float32),
            jax.ShapeDtypeStruct((t, d), jnp.bfloat16),
            jax.ShapeDtypeStruct((t, LANES), jnp.float32),
        ],
        compiler_params=_params("parallel"),
        name="outproj_router",
    )(x, u, a, w_u, w_a, g.reshape(1, d), w_r)


def _gate_up_kernel(x_ref, wg_ref, wu_ref, o_ref):
    x = x_ref[0]
    g = jnp.dot(x, wg_ref[0, 0].astype(x.dtype), preferred_element_type=jnp.float32)
    u = jnp.dot(x, wu_ref[0, 0].astype(x.dtype), preferred_element_type=jnp.float32)
    o_ref[0] = (g * jax.nn.sigmoid(g) * u).astype(o_ref.dtype)


def _expert_gate_up(xe, w_gate, w_up, layer):
    e, m, d = xe.shape
    f = w_gate.shape[3]
    tm, tn = _tile(m, 1024), _tile(f, 512)
    return pl.pallas_call(
        _gate_up_kernel,
        grid=(e, f // tn, m // tm),
        in_specs=[
            pl.BlockSpec((1, tm, d), lambda ei, n, mi: (ei, mi, 0)),
            pl.BlockSpec((1, 1, d, tn), lambda ei, n, mi: (layer, ei, 0, n)),
            pl.BlockSpec((1, 1, d, tn), lambda ei, n, mi: (layer, ei, 0, n)),
        ],
        out_specs=pl.BlockSpec((1, tm, tn), lambda ei, n, mi: (ei, mi, n)),
        out_shape=jax.ShapeDtypeStruct((e, m, f), jnp.bfloat16),
        compiler_params=_params("parallel", "parallel", "arbitrary"),
        name="expert_gate_up",
    )(xe, w_gate, w_up)


def _down_kernel(h_ref, w_ref, gate_ref, o_ref):
    h = h_ref[0]
    y = jnp.dot(h, w_ref[0, 0].astype(h.dtype), preferred_element_type=jnp.float32)
    reps = y.shape[1] // LANES
    o_ref[0] = y * jnp.concatenate([gate_ref[0]] * reps, axis=1)


def _expert_down(h, w_down, gates_b, layer):
    e, m, f = h.shape
    d = w_down.shape[3]
    tm, tn = _tile(m, 1024), _tile(d, 256)
    return pl.pallas_call(
        _down_kernel,
        grid=(e, m // tm, d // tn),
        in_specs=[
            pl.BlockSpec((1, tm, f), lambda ei, mi, n: (ei, mi, 0)),
            pl.BlockSpec((1, 1, f, tn), lambda ei, mi, n: (layer, ei, 0, n)),
            pl.BlockSpec((1, tm, LANES), lambda ei, mi, n: (ei, mi, 0)),
        ],
        out_specs=pl.BlockSpec((1, tm, tn), lambda ei, mi, n: (ei, mi, n)),
        out_shape=jax.ShapeDtypeStruct((e, m, d), jnp.float32),
        compiler_params=_params("parallel", "parallel", "arbitrary"),
        name="expert_down",
    )(h, w_down, gates_b)


def _rmsnorm_kernel(x_ref, g_ref, o_ref):
    x = x_ref[...]
    ms = jnp.mean(x * x, axis=-1, keepdims=True)
    o_ref[...] = x * lax.rsqrt(ms + EPS) * g_ref[...]


def _rmsnorm(x, g):
    t, d = x.shape
    tm = _tile(t, 512)
    return pl.pallas_call(
        _rmsnorm_kernel,
        grid=(t // tm,),
        in_specs=[pl.BlockSpec((tm, d), lambda i: (i, 0)), pl.BlockSpec((1, d), lambda i: (0, 0))],
        out_specs=pl.BlockSpec((tm, d), lambda i: (i, 0)),
        out_shape=jax.ShapeDtypeStruct((t, d), jnp.float32),
        compiler_params=_params("parallel"),
        name="final_rmsnorm",
    )(x, g.reshape(1, d))


def _route(aff, h, n_experts):
    n = aff.shape[0]
    cap = CAPACITY_FACTOR * n // n_experts
    gates, idx = lax.top_k(aff[:, :n_experts].T, cap)
    return gates, idx, h[idx]


def kernel(x_prompt, x_sample, rel_bias, final_g, ln1_g, w_in, conv_w, conv_b, conv_ln_g, conv_ln_b,
           lam_q1, lam_k1, lam_q2, lam_k2, subln_g, w_out, ln2_g, w_router, w_gate, w_up, w_down):
    bp, sp, d = x_prompt.shape
    bs, ss, _ = x_sample.shape
    n_p, n_s = bp * sp, bs * ss
    depth = w_in.shape[0]
    cw = conv_w.shape[2]
    n_heads = (w_in.shape[2] - 2 * cw) // (3 * V_DIM)
    n_experts = w_router.shape[2]
    bf16 = jnp.bfloat16

    x = jnp.concatenate([x_prompt.reshape(n_p, d), x_sample.reshape(n_s, d)], axis=0)
    tq_attn = _tile(math.gcd(sp, ss), 512)
    tk_attn = _tile(tq_attn, 256)
    bias_tiles = _bias_tiles(rel_bias, tq_attn, tk_attn)
    qk_w = n_heads * V_DIM
    qkv_scale = jnp.concatenate([jnp.full((qk_w,), LOG2E * HEAD_DIM ** -0.5, jnp.float32),
                                 jnp.ones((2 * qk_w,), jnp.float32)])

    for l in range(depth):
        w_in_l = w_in[l].astype(bf16)
        zc = _norm_matmul(x, ln1_g[l], w_in_l[:, :2 * cw], jnp.ones((2 * cw,), jnp.float32), jnp.float32)
        zqkv = _norm_matmul(x, ln1_g[l], w_in_l[:, 2 * cw:], qkv_scale, bf16)

        u = _conv_module(zc, conv_w[l], conv_b[l], conv_ln_g[l], conv_ln_b[l], n_p, sp, ss)

        lam_init = 0.8 - 0.6 * math.exp(-0.3 * l)
        lam = (jnp.exp(jnp.sum(lam_q1[l] * lam_k1[l])) - jnp.exp(jnp.sum(lam_q2[l] * lam_k2[l])) + lam_init)
        vt = zqkv[:, 2 * qk_w:].reshape((n_p + n_s) // tk_attn, tk_attn, n_heads, V_DIM).transpose(2, 0, 3, 1)
        attn = functools.partial(_diff_attention, zqkv, vt, n_heads=n_heads, lam=lam, bias_tiles=bias_tiles,
                                 subln_g=subln_g[l], post_scale=1.0 - lam_init)
        a = jnp.concatenate([attn(row0=0, batch=bp, seq=sp), attn(row0=n_p, batch=bs, seq=ss)], axis=0)

        w_out_l = w_out[l].astype(bf16)
        w_r = jnp.pad(w_router[l], ((0, 0), (0, LANES - n_experts))).astype(bf16)
        x1, h, aff = _outproj_router(x, u, a, w_out_l[:cw], w_out_l[cw:], ln2_g[l], w_r, n_experts)

        gates_p, idx_p, xe_p = _route(aff[:n_p], h[:n_p], n_experts)
        gates_s, idx_s, xe_s = _route(aff[n_p:], h[n_p:], n_experts)
        xe = jnp.concatenate([xe_p, xe_s], axis=1)
        gates = jnp.concatenate([gates_p, gates_s], axis=1)
        idx = jnp.concatenate([idx_p, idx_s + n_p], axis=1)

        hid = _expert_gate_up(xe, w_gate, w_up, l)
        gates_b = jnp.broadcast_to(gates[..., None], gates.shape + (LANES,))
        ye = _expert_down(hid, w_down, gates_b, l)
        x = x1.at[idx.reshape(-1)].add(ye.reshape(-1, d))

    y = _rmsnorm(x, final_g)
    return y[:n_p].reshape(bp, sp, d), y[n_p:].reshape(bs, ss, d)
```

```python
import functools
import math

import jax
import jax.numpy as jnp
from jax import lax
from jax.experimental import pallas as pl
from jax.experimental.pallas import tpu as pltpu

EPS = 1e-6
CONV_KERNEL = 31
CONV_PAD = CONV_KERNEL // 2
HALO = 16
HEAD_DIM = 64
V_DIM = 2 * HEAD_DIM
N_BUCKETS = 32
MAX_DISTANCE = 128
CAPACITY_FACTOR = 2
LANES = 128
SUBLANES = 8
KV_GROUP = 8
ONES_ROWS = 16
LOG2E = math.log2(math.e)
VMEM_LIMIT_BYTES = 56 * 1024 * 1024

_NT = (((1,), (1,)), ((), ()))


def _tile(n, pref):
    t = min(n, pref)
    while n % t:
        t -= 1
    return t


def _params(*sem):
    return pltpu.CompilerParams(dimension_semantics=sem, vmem_limit_bytes=VMEM_LIMIT_BYTES)


def _norm_matmul_kernel(x_ref, g_ref, w_ref, s_ref, o_ref, h_ref):
    @pl.when(pl.program_id(1) == 0)
    def _():
        x = x_ref[...]
        ms = jnp.mean(x * x, axis=-1, keepdims=True)
        h_ref[...] = (x * lax.rsqrt(ms + EPS) * g_ref[...]).astype(h_ref.dtype)

    y = jnp.dot(h_ref[...], w_ref[...], preferred_element_type=jnp.float32)
    o_ref[...] = (y * s_ref[...]).astype(o_ref.dtype)


def _norm_matmul(x, g, w, col_scale, out_dtype):
    t, d = x.shape
    n = w.shape[1]
    tm, tn = _tile(t, 1024), _tile(n, 1024)
    return pl.pallas_call(
        _norm_matmul_kernel,
        grid=(t // tm, n // tn),
        in_specs=[
            pl.BlockSpec((tm, d), lambda i, j: (i, 0)),
            pl.BlockSpec((1, d), lambda i, j: (0, 0)),
            pl.BlockSpec((d, tn), lambda i, j: (0, j)),
            pl.BlockSpec((1, tn), lambda i, j: (0, j)),
        ],
        out_specs=pl.BlockSpec((tm, tn), lambda i, j: (i, j)),
        out_shape=jax.ShapeDtypeStruct((t, n), out_dtype),
        scratch_shapes=[pltpu.VMEM((tm, d), jnp.bfloat16)],
        compiler_params=_params("parallel", "arbitrary"),
        name="norm_matmul",
    )(x, g.reshape(1, d), w, col_scale.reshape(1, n))


def _conv_kernel(prev_ref, cur_ref, next_ref, w_ref, b_ref, lg_ref, lb_ref, o_ref, e_ref, es_ref, c_ref,
                 *, ts, cw, n_p, s_p, s_s):
    r0 = pl.program_id(0) * ts
    in_p = r0 < n_p
    local = jnp.where(in_p, r0 % s_p, (r0 - n_p) % s_s)
    slen = jnp.where(in_p, s_p, s_s)
    keep_prev = (local != 0).astype(jnp.float32)
    keep_next = (local + ts != slen).astype(jnp.float32)

    def glu(z):
        return z[:, :cw] * jax.nn.sigmoid(z[:, cw:])

    e_ref[0:HALO, :] = glu(prev_ref[...]) * keep_prev
    e_ref[HALO:HALO + ts, :] = glu(cur_ref[...])
    e_ref[HALO + ts:, :] = glu(next_ref[...]) * keep_next

    off = HALO - CONV_PAD
    rows = es_ref.shape[1]
    for r in range(1, SUBLANES):
        es_ref[r - 1] = e_ref[r:r + rows, :]
    for c in range(cw // LANES):
        cs = slice(c * LANES, (c + 1) * LANES)
        acc = jnp.broadcast_to(b_ref[:, cs], (ts, LANES))
        for r in range(SUBLANES):
            x = e_ref[0:rows, cs] if r == 0 else es_ref[r - 1, :, cs]
            for base in range(0, rows - ts + 1, SUBLANES):
                j = base + r - off
                if 0 <= j < CONV_KERNEL:
                    acc = acc + w_ref[j:j + 1, cs] * x[base:base + ts]
        c_ref[:, cs] = acc

    u = c_ref[...]
    mu = jnp.mean(u, axis=-1, keepdims=True)
    var = jnp.mean(jnp.square(u - mu), axis=-1, keepdims=True)
    y = (u - mu) * lax.rsqrt(var + EPS) * lg_ref[...] + lb_ref[...]
    o_ref[...] = (y * jax.nn.sigmoid(y)).astype(o_ref.dtype)


def _conv_module(zc, conv_w, conv_b, ln_g, ln_b, n_p, s_p, s_s):
    t = zc.shape[0]
    cw = zc.shape[1] // 2
    ts = _tile(math.gcd(s_p, s_s), 256)
    hb = ts // HALO
    last_hb = t // HALO - 1
    shifted_extra = (HALO - CONV_PAD + CONV_KERNEL - 1) // SUBLANES * SUBLANES
    kern = functools.partial(_conv_kernel, ts=ts, cw=cw, n_p=n_p, s_p=s_p, s_s=s_s)
    return pl.pallas_call(
        kern,
        grid=(t // ts,),
        in_specs=[
            pl.BlockSpec((HALO, 2 * cw), lambda i: (jnp.maximum(i * hb - 1, 0), 0)),
            pl.BlockSpec((ts, 2 * cw), lambda i: (i, 0)),
            pl.BlockSpec((HALO, 2 * cw), lambda i: (jnp.minimum((i + 1) * hb, last_hb), 0)),
            pl.BlockSpec((CONV_KERNEL, cw), lambda i: (0, 0)),
            pl.BlockSpec((1, cw), lambda i: (0, 0)),
            pl.BlockSpec((1, cw), lambda i: (0, 0)),
            pl.BlockSpec((1, cw), lambda i: (0, 0)),
        ],
        out_specs=pl.BlockSpec((ts, cw), lambda i: (i, 0)),
        out_shape=jax.ShapeDtypeStruct((t, cw), jnp.bfloat16),
        scratch_shapes=[pltpu.VMEM((ts + 2 * HALO, cw), jnp.float32),
                        pltpu.VMEM((SUBLANES - 1, ts + shifted_extra, cw), jnp.float32),
                        pltpu.VMEM((ts, cw), jnp.float32)],
        compiler_params=_params("parallel"),
        name="conv_module",
    )(zc, zc, zc, conv_w, conv_b.reshape(1, cw), ln_g.reshape(1, cw), ln_b.reshape(1, cw))


def _rel_bucket(rel):
    half = N_BUCKETS // 2
    max_exact = half // 2
    ret = (rel > 0).astype(jnp.int32) * half
    n = jnp.abs(rel)
    nf = jnp.maximum(n, 1).astype(jnp.float32)
    large = max_exact + (jnp.log(nf / max_exact) / math.log(MAX_DISTANCE / max_exact)
                         * (half - max_exact)).astype(jnp.int32)
    large = jnp.minimum(large, half - 1)
    return ret + jnp.where(n < max_exact, n, large)


def _bias_tiles(rel_bias, tq, tk):
    assert tk > MAX_DISTANCE and tq % tk == 0
    d = jnp.arange(-2, tq // tk + 2, dtype=jnp.int32)[:, None, None]
    kk = jnp.arange(tk, dtype=jnp.int32)[None, :, None]
    qq = jnp.arange(tq, dtype=jnp.int32)[None, None, :]
    buckets = _rel_bucket(d * tk + kk - qq)[None]
    table = rel_bias.astype(jnp.float32).T * LOG2E
    out = jnp.zeros((table.shape[0],) + buckets.shape[1:], jnp.float32)
    for bkt in range(N_BUCKETS):
        out = jnp.where(buckets == bkt, table[:, bkt][:, None, None, None], out)
    return out


def _attn_kernel(lam_ref, q_ref, k_ref, vt_ref, bias_ref, g_ref, o_ref, qm_ref, st_ref, sc_ref,
                 *, tq, tk, post_scale):
    i = pl.program_id(1)
    nkv = k_ref.shape[0] // tk
    strips = tq // tk
    q = q_ref[...]
    lane = lax.broadcasted_iota(jnp.int32, q.shape, 1)
    zero = jnp.zeros_like(q)
    qm_ref[0] = jnp.where(lane < HEAD_DIM, q, zero)
    qm_ref[1] = jnp.where(lane >= HEAD_DIM, q, zero)
    r_m, r_l = V_DIM, V_DIM + SUBLANES
    st_ref[:, :r_m, :] = jnp.zeros((2, r_m, tq), jnp.float32)
    st_ref[:, r_m:r_l, :] = jnp.full((2, SUBLANES, tq), -jnp.inf, jnp.float32)
    st_ref[:, r_l:, :] = jnp.zeros((2, SUBLANES, tq), jnp.float32)

    chains = [(mp, slice(s_i * tk, (s_i + 1) * tk)) for mp in range(2) for s_i in range(strips)]

    def scores(c):
        k = k_ref[pl.ds(pl.multiple_of(c * tk, tk), tk), :]
        return jnp.stack([lax.dot_general(k, qm_ref[mp, qs, :], _NT, preferred_element_type=jnp.float32)
                          for mp, qs in chains])

    sc_ref[0] = scores(0)

    def body(c, _):
        nxt = scores(jnp.minimum(c + 1, nkv - 1))
        cur = sc_ref[c % 2]
        vt = vt_ref[0, c]
        tile = jnp.clip(c - i * strips, -2, strips + 1) + 2
        st = st_ref[...]
        stats, probs = [], []
        for n, (mp, qs) in enumerate(chains):
            s = cur[n] + bias_ref[0, tile, :, qs]
            m_prev = st[mp, r_m:r_l, qs]
            m_new = jnp.maximum(m_prev, jnp.max(s, axis=0, keepdims=True))
            stats.append((jnp.exp2(m_prev - m_new), m_new))
            probs.append(jnp.exp2(s - m_new[:1]).astype(vt.dtype))
        blocks = []
        for (mp, qs), p, (alpha, m_new) in zip(chains, probs, stats):
            pv = jnp.dot(vt, p, preferred_element_type=jnp.float32)
            a_new = alpha[:1] * st[mp, :r_m, qs] + pv[:r_m]
            l_new = alpha * st[mp, r_l:, qs] + pv[r_m:r_l]
            blocks.append(jnp.concatenate([a_new, m_new, l_new], axis=0))
        st_ref[...] = jnp.stack([jnp.concatenate(blocks[mp * strips:(mp + 1) * strips], axis=1)
                                 for mp in range(2)])
        sc_ref[(c + 1) % 2] = nxt
        return 0

    lax.fori_loop(0, nkv, body, 0, unroll=_tile(nkv, KV_GROUP))

    l1, l2 = st_ref[0, r_l:r_l + 1, :], st_ref[1, r_l:r_l + 1, :]
    o = st_ref[0, :r_m, :] / l1 - lam_ref[0] * (st_ref[1, :r_m, :] / l2)
    ms = jnp.mean(o * o, axis=0, keepdims=True)
    o = (o * lax.rsqrt(ms + EPS)).T * (g_ref[...] * post_scale)
    o_ref[...] = o.astype(o_ref.dtype)


def _diff_attention(zqkv, vt, row0, batch, seq, n_heads, lam, bias_tiles, subln_g, post_scale):
    _, n_tiles, tk, tq = bias_tiles.shape
    nt = seq // tq
    rb0 = row0 // tq
    kern = functools.partial(_attn_kernel, tq=tq, tk=tk, post_scale=post_scale)
    return pl.pallas_call(
        kern,
        grid=(batch * n_heads, nt),
        in_specs=[
            pl.BlockSpec(memory_space=pltpu.SMEM),
            pl.BlockSpec((tq, V_DIM), lambda bh, i: (rb0 + (bh // n_heads) * nt + i, bh % n_heads)),
            pl.BlockSpec((seq, V_DIM), lambda bh, i: (row0 // seq + bh // n_heads, n_heads + bh % n_heads)),
            pl.BlockSpec((1, seq // tk, vt.shape[2], tk),
                         lambda bh, i: (bh % n_heads, row0 // seq + bh // n_heads, 0, 0)),
            pl.BlockSpec((1, n_tiles, tk, tq), lambda bh, i: (bh % n_heads, 0, 0, 0)),
            pl.BlockSpec((1, V_DIM), lambda bh, i: (0, 0)),
        ],
        out_specs=pl.BlockSpec((tq, V_DIM), lambda bh, i: ((bh // n_heads) * nt + i, bh % n_heads)),
        out_shape=jax.ShapeDtypeStruct((batch * seq, n_heads * V_DIM), jnp.bfloat16),
        scratch_shapes=[pltpu.VMEM((2, tq, V_DIM), jnp.bfloat16),
                        pltpu.VMEM((2, V_DIM + 2 * SUBLANES, tq), jnp.float32),
                        pltpu.VMEM((2, 2 * tq // tk, tk, tk), jnp.float32)],
        compiler_params=_params("parallel", "arbitrary"),
        name="diff_attention",
    )(lam.reshape(1), zqkv, zqkv, vt, bias_tiles, subln_g.reshape(1, V_DIM))


def _outproj_kernel(x_ref, u_ref, a_ref, wu_ref, wa_ref, g_ref, wr_ref, x1_ref, h_ref, aff_ref, *, n_experts):
    y = jnp.dot(u_ref[...], wu_ref[...], preferred_element_type=jnp.float32)
    y = y + jnp.dot(a_ref[...], wa_ref[...], preferred_element_type=jnp.float32)
    x1 = x_ref[...] + y
    x1_ref[...] = x1
    ms = jnp.mean(x1 * x1, axis=-1, keepdims=True)
    h = (x1 * lax.rsqrt(ms + EPS) * g_ref[...]).astype(h_ref.dtype)
    h_ref[...] = h
    logits = jnp.dot(h, wr_ref[...], preferred_element_type=jnp.float32)
    lane = lax.broadcasted_iota(jnp.int32, logits.shape, 1)
    logits = jnp.where(lane < n_experts, logits, -jnp.inf)
    e = jnp.exp(logits - jnp.max(logits, axis=-1, keepdims=True))
    aff_ref[...] = e / jnp.sum(e, axis=-1, keepdims=True)


def _outproj_router(x, u, a, w_u, w_a, g, w_r, n_experts):
    t, d = x.shape
    cw, aw = u.shape[1], a.shape[1]
    tm = _tile(t, 256)
    kern = functools.partial(_outproj_kernel, n_experts=n_experts)
    return pl.pallas_call(
        kern,
        grid=(t // tm,),
        in_specs=[
            pl.BlockSpec((tm, d), lambda i: (i, 0)),
            pl.BlockSpec((tm, cw), lambda i: (i, 0)),
            pl.BlockSpec((tm, aw), lambda i: (i, 0)),
            pl.BlockSpec((cw, d), lambda i: (0, 0)),
            pl.BlockSpec((aw, d), lambda i: (0, 0)),
            pl.BlockSpec((1, d), lambda i: (0, 0)),
            pl.BlockSpec((d, LANES), lambda i: (0, 0)),
        ],
        out_specs=[
            pl.BlockSpec((tm, d), lambda i: (i, 0)),
            pl.BlockSpec((tm, d), lambda i: (i, 0)),
            pl.BlockSpec((tm, LANES), lambda i: (i, 0)),
        ],
        out_shape=[
            jax.ShapeDtypeStruct((t, d), jnp.float32),
            jax.ShapeDtypeStruct((t, d), jnp.bfloat16),
            jax.ShapeDtypeStruct((t, LANES), jnp.float32),
        ],
        compiler_params=_params("parallel"),
        name="outproj_router",
    )(x, u, a, w_u, w_a, g.reshape(1, d), w_r)


def _gate_up_kernel(x_ref, wg_ref, wu_ref, o_ref):
    x = x_ref[0]
    g = jnp.dot(x, wg_ref[0, 0].astype(x.dtype), preferred_element_type=jnp.float32)
    u = jnp.dot(x, wu_ref[0, 0].astype(x.dtype), preferred_element_type=jnp.float32)
    o_ref[0] = (g * jax.nn.sigmoid(g) * u).astype(o_ref.dtype)


def _expert_gate_up(xe, w_gate, w_up, layer):
    e, m, d = xe.shape
    f = w_gate.shape[3]
    tm, tn = _tile(m, 1024), _tile(f, 512)
    return pl.pallas_call(
        _gate_up_kernel,
        grid=(e, f // tn, m // tm),
        in_specs=[
            pl.BlockSpec((1, tm, d), lambda ei, n, mi: (ei, mi, 0)),
            pl.BlockSpec((1, 1, d, tn), lambda ei, n, mi: (layer, ei, 0, n)),
            pl.BlockSpec((1, 1, d, tn), lambda ei, n, mi: (layer, ei, 0, n)),
        ],
        out_specs=pl.BlockSpec((1, tm, tn), lambda ei, n, mi: (ei, mi, n)),
        out_shape=jax.ShapeDtypeStruct((e, m, f), jnp.bfloat16),
        compiler_params=_params("parallel", "parallel", "arbitrary"),
        name="expert_gate_up",
    )(xe, w_gate, w_up)


def _down_kernel(h_ref, w_ref, gate_ref, o_ref):
    h = h_ref[0]
    y = jnp.dot(h, w_ref[0, 0].astype(h.dtype), preferred_element_type=jnp.float32)
    reps = y.shape[1] // LANES
    o_ref[0] = y * jnp.concatenate([gate_ref[0]] * reps, axis=1)


def _expert_down(h, w_down, gates_b, layer):
    e, m, f = h.shape
    d = w_down.shape[3]
    tm, tn = _tile(m, 1024), _tile(d, 256)
    return pl.pallas_call(
        _down_kernel,
        grid=(e, m // tm, d // tn),
        in_specs=[
            pl.BlockSpec((1, tm, f), lambda ei, mi, n: (ei, mi, 0)),
            pl.BlockSpec((1, 1, f, tn), lambda ei, mi, n: (layer, ei, 0, n)),
            pl.BlockSpec((1, tm, LANES), lambda ei, mi, n: (ei, mi, 0)),
        ],
        out_specs=pl.BlockSpec((1, tm, tn), lambda ei, mi, n: (ei, mi, n)),
        out_shape=jax.ShapeDtypeStruct((e, m, d), jnp.float32),
        compiler_params=_params("parallel", "parallel", "arbitrary"),
        name="expert_down",
    )(h, w_down, gates_b)


def _rmsnorm_kernel(x_ref, g_ref, o_ref):
    x = x_ref[...]
    ms = jnp.mean(x * x, axis=-1, keepdims=True)
    o_ref[...] = x * lax.rsqrt(ms + EPS) * g_ref[...]


def _rmsnorm(x, g):
    t, d = x.shape
    tm = _tile(t, 512)
    return pl.pallas_call(
        _rmsnorm_kernel,
        grid=(t // tm,),
        in_specs=[pl.BlockSpec((tm, d), lambda i: (i, 0)), pl.BlockSpec((1, d), lambda i: (0, 0))],
        out_specs=pl.BlockSpec((tm, d), lambda i: (i, 0)),
        out_shape=jax.ShapeDtypeStruct((t, d), jnp.float32),
        compiler_params=_params("parallel"),
        name="final_rmsnorm",
    )(x, g.reshape(1, d))


def _route(aff, h, n_experts):
    n = aff.shape[0]
    cap = CAPACITY_FACTOR * n // n_experts
    gates, idx = lax.top_k(aff[:, :n_experts].T, cap)
    return gates, idx, h[idx]


def kernel(x_prompt, x_sample, rel_bias, final_g, ln1_g, w_in, conv_w, conv_b, conv_ln_g, conv_ln_b,
           lam_q1, lam_k1, lam_q2, lam_k2, subln_g, w_out, ln2_g, w_router, w_gate, w_up, w_down):
    bp, sp, d = x_prompt.shape
    bs, ss, _ = x_sample.shape
    n_p, n_s = bp * sp, bs * ss
    depth = w_in.shape[0]
    cw = conv_w.shape[2]
    n_heads = (w_in.shape[2] - 2 * cw) // (3 * V_DIM)
    n_experts = w_router.shape[2]
    bf16 = jnp.bfloat16

    x = jnp.concatenate([x_prompt.reshape(n_p, d), x_sample.reshape(n_s, d)], axis=0)
    tq_attn = _tile(math.gcd(sp, ss), 512)
    tk_attn = _tile(tq_attn, 256)
    bias_tiles = _bias_tiles(rel_bias, tq_attn, tk_attn)
    qk_w = n_heads * V_DIM
    qkv_scale = jnp.concatenate([jnp.full((qk_w,), LOG2E * HEAD_DIM ** -0.5, jnp.float32),
                                 jnp.ones((2 * qk_w,), jnp.float32)])

    for l in range(depth):
        w_in_l = w_in[l].astype(bf16)
        zc = _norm_matmul(x, ln1_g[l], w_in_l[:, :2 * cw], jnp.ones((2 * cw,), jnp.float32), jnp.float32)
        zqkv = _norm_matmul(x, ln1_g[l], w_in_l[:, 2 * cw:], qkv_scale, bf16)

        u = _conv_module(zc, conv_w[l], conv_b[l], conv_ln_g[l], conv_ln_b[l], n_p, sp, ss)

        lam_init = 0.8 - 0.6 * math.exp(-0.3 * l)
        lam = (jnp.exp(jnp.sum(lam_q1[l] * lam_k1[l])) - jnp.exp(jnp.sum(lam_q2[l] * lam_k2[l])) + lam_init)
        vt = zqkv[:, 2 * qk_w:].reshape((n_p + n_s) // tk_attn, tk_attn, n_heads, V_DIM).transpose(2, 0, 3, 1)
        vt = jnp.concatenate([vt, jnp.ones(vt.shape[:2] + (ONES_ROWS, tk_attn), bf16)], axis=2)
        attn = functools.partial(_diff_attention, zqkv, vt, n_heads=n_heads, lam=lam, bias_tiles=bias_tiles,
                                 subln_g=subln_g[l], post_scale=1.0 - lam_init)
        a = jnp.concatenate([attn(row0=0, batch=bp, seq=sp), attn(row0=n_p, batch=bs, seq=ss)], axis=0)

        w_out_l = w_out[l].astype(bf16)
        w_r = jnp.pad(w_router[l], ((0, 0), (0, LANES - n_experts))).astype(bf16)
        x1, h, aff = _outproj_router(x, u, a, w_out_l[:cw], w_out_l[cw:], ln2_g[l], w_r, n_experts)

        gates_p, idx_p, xe_p = _route(aff[:n_p], h[:n_p], n_experts)
        gates_s, idx_s, xe_s = _route(aff[n_p:], h[n_p:], n_experts)
        xe = jnp.concatenate([xe_p, xe_s], axis=1)
        gates = jnp.concatenate([gates_p, gates_s], axis=1)
        idx = jnp.concatenate([idx_p, idx_s + n_p], axis=1)

        hid = _expert_gate_up(xe, w_gate, w_up, l)
        gates_b = jnp.broadcast_to(gates[..., None], gates.shape + (LANES,))
        ye = _expert_down(hid, w_down, gates_b, l)
        x = x1.at[idx.reshape(-1)].add(ye.reshape(-1, d))

    y = _rmsnorm(x, final_g)
    return y[:n_p].reshape(bp, sp, d), y[n_p:].reshape(bs, ss, d)
```

```python
import functools
import math

import jax
import jax.numpy as jnp
from jax import lax
from jax.experimental import pallas as pl
from jax.experimental.pallas import tpu as pltpu

EPS = 1e-6
CONV_KERNEL = 31
CONV_PAD = CONV_KERNEL // 2
HALO = 16
HEAD_DIM = 64
V_DIM = 2 * HEAD_DIM
N_BUCKETS = 32
MAX_DISTANCE = 128
CAPACITY_FACTOR = 2
LANES = 128
SUBLANES = 8
EXPERT_GROUP = 4
KV_GROUP = 8
ONES_ROWS = 16
LOG2E = math.log2(math.e)
VMEM_LIMIT_BYTES = 56 * 1024 * 1024

_NT = (((1,), (1,)), ((), ()))


def _tile(n, pref):
    t = min(n, pref)
    while n % t:
        t -= 1
    return t


def _params(*sem):
    return pltpu.CompilerParams(dimension_semantics=sem, vmem_limit_bytes=VMEM_LIMIT_BYTES)


def _norm_matmul_kernel(x_ref, g_ref, w_ref, s_ref, o_ref, h_ref):
    @pl.when(pl.program_id(1) == 0)
    def _():
        x = x_ref[...]
        ms = jnp.mean(x * x, axis=-1, keepdims=True)
        h_ref[...] = (x * lax.rsqrt(ms + EPS) * g_ref[...]).astype(h_ref.dtype)

    y = jnp.dot(h_ref[...], w_ref[...], preferred_element_type=jnp.float32)
    o_ref[...] = (y * s_ref[...]).astype(o_ref.dtype)


def _norm_matmul(x, g, w, col_scale, out_dtype):
    t, d = x.shape
    n = w.shape[1]
    tm, tn = _tile(t, 1024), _tile(n, 1024)
    return pl.pallas_call(
        _norm_matmul_kernel,
        grid=(t // tm, n // tn),
        in_specs=[
            pl.BlockSpec((tm, d), lambda i, j: (i, 0)),
            pl.BlockSpec((1, d), lambda i, j: (0, 0)),
            pl.BlockSpec((d, tn), lambda i, j: (0, j)),
            pl.BlockSpec((1, tn), lambda i, j: (0, j)),
        ],
        out_specs=pl.BlockSpec((tm, tn), lambda i, j: (i, j)),
        out_shape=jax.ShapeDtypeStruct((t, n), out_dtype),
        scratch_shapes=[pltpu.VMEM((tm, d), jnp.bfloat16)],
        compiler_params=_params("parallel", "arbitrary"),
        name="norm_matmul",
    )(x, g.reshape(1, d), w, col_scale.reshape(1, n))


def _conv_kernel(prev_ref, cur_ref, next_ref, w_ref, b_ref, lg_ref, lb_ref, o_ref, e_ref, es_ref, c_ref,
                 *, ts, cw, n_p, s_p, s_s):
    r0 = pl.program_id(0) * ts
    in_p = r0 < n_p
    local = jnp.where(in_p, r0 % s_p, (r0 - n_p) % s_s)
    slen = jnp.where(in_p, s_p, s_s)
    keep_prev = (local != 0).astype(jnp.float32)
    keep_next = (local + ts != slen).astype(jnp.float32)

    def glu(z):
        return z[:, :cw] * jax.nn.sigmoid(z[:, cw:])

    e_ref[0:HALO, :] = glu(prev_ref[...]) * keep_prev
    e_ref[HALO:HALO + ts, :] = glu(cur_ref[...])
    e_ref[HALO + ts:, :] = glu(next_ref[...]) * keep_next

    off = HALO - CONV_PAD
    rows = es_ref.shape[1]
    for r in range(1, SUBLANES):
        es_ref[r - 1] = e_ref[r:r + rows, :]
    for c in range(cw // LANES):
        cs = slice(c * LANES, (c + 1) * LANES)
        acc = jnp.broadcast_to(b_ref[:, cs], (ts, LANES))
        for r in range(SUBLANES):
            x = e_ref[0:rows, cs] if r == 0 else es_ref[r - 1, :, cs]
            for base in range(0, rows - ts + 1, SUBLANES):
                j = base + r - off
                if 0 <= j < CONV_KERNEL:
                    acc = acc + w_ref[j:j + 1, cs] * x[base:base + ts]
        c_ref[:, cs] = acc

    u = c_ref[...]
    mu = jnp.mean(u, axis=-1, keepdims=True)
    var = jnp.mean(jnp.square(u - mu), axis=-1, keepdims=True)
    y = (u - mu) * lax.rsqrt(var + EPS) * lg_ref[...] + lb_ref[...]
    o_ref[...] = (y * jax.nn.sigmoid(y)).astype(o_ref.dtype)


def _conv_module(zc, conv_w, conv_b, ln_g, ln_b, n_p, s_p, s_s):
    t = zc.shape[0]
    cw = zc.shape[1] // 2
    ts = _tile(math.gcd(s_p, s_s), 256)
    hb = ts // HALO
    last_hb = t // HALO - 1
    shifted_extra = (HALO - CONV_PAD + CONV_KERNEL - 1) // SUBLANES * SUBLANES
    kern = functools.partial(_conv_kernel, ts=ts, cw=cw, n_p=n_p, s_p=s_p, s_s=s_s)
    return pl.pallas_call(
        kern,
        grid=(t // ts,),
        in_specs=[
            pl.BlockSpec((HALO, 2 * cw), lambda i: (jnp.maximum(i * hb - 1, 0), 0)),
            pl.BlockSpec((ts, 2 * cw), lambda i: (i, 0)),
            pl.BlockSpec((HALO, 2 * cw), lambda i: (jnp.minimum((i + 1) * hb, last_hb), 0)),
            pl.BlockSpec((CONV_KERNEL, cw), lambda i: (0, 0)),
            pl.BlockSpec((1, cw), lambda i: (0, 0)),
            pl.BlockSpec((1, cw), lambda i: (0, 0)),
            pl.BlockSpec((1, cw), lambda i: (0, 0)),
        ],
        out_specs=pl.BlockSpec((ts, cw), lambda i: (i, 0)),
        out_shape=jax.ShapeDtypeStruct((t, cw), jnp.bfloat16),
        scratch_shapes=[pltpu.VMEM((ts + 2 * HALO, cw), jnp.float32),
                        pltpu.VMEM((SUBLANES - 1, ts + shifted_extra, cw), jnp.float32),
                        pltpu.VMEM((ts, cw), jnp.float32)],
        compiler_params=_params("parallel"),
        name="conv_module",
    )(zc, zc, zc, conv_w, conv_b.reshape(1, cw), ln_g.reshape(1, cw), ln_b.reshape(1, cw))


def _rel_bucket(rel):
    half = N_BUCKETS // 2
    max_exact = half // 2
    ret = (rel > 0).astype(jnp.int32) * half
    n = jnp.abs(rel)
    nf = jnp.maximum(n, 1).astype(jnp.float32)
    large = max_exact + (jnp.log(nf / max_exact) / math.log(MAX_DISTANCE / max_exact)
                         * (half - max_exact)).astype(jnp.int32)
    large = jnp.minimum(large, half - 1)
    return ret + jnp.where(n < max_exact, n, large)


def _bias_tiles(rel_bias, tq, tk):
    assert tk > MAX_DISTANCE and tq % tk == 0
    d = jnp.arange(-2, tq // tk + 2, dtype=jnp.int32)[:, None, None]
    kk = jnp.arange(tk, dtype=jnp.int32)[None, :, None]
    qq = jnp.arange(tq, dtype=jnp.int32)[None, None, :]
    buckets = _rel_bucket(d * tk + kk - qq)[None]
    table = rel_bias.astype(jnp.float32).T * LOG2E
    out = jnp.zeros((table.shape[0],) + buckets.shape[1:], jnp.float32)
    for bkt in range(N_BUCKETS):
        out = jnp.where(buckets == bkt, table[:, bkt][:, None, None, None], out)
    return out


def _attn_kernel(lam_ref, q_ref, k_ref, vt_ref, bias_ref, g_ref, o_ref, qm_ref, st_ref, sc_ref,
                 *, tq, tk, post_scale):
    i = pl.program_id(1)
    nkv = k_ref.shape[0] // tk
    strips = tq // tk
    q = q_ref[...]
    lane = lax.broadcasted_iota(jnp.int32, q.shape, 1)
    zero = jnp.zeros_like(q)
    qm_ref[0] = jnp.where(lane < HEAD_DIM, q, zero)
    qm_ref[1] = jnp.where(lane >= HEAD_DIM, q, zero)
    r_m, r_l = V_DIM, V_DIM + SUBLANES
    st_ref[:, :r_m, :] = jnp.zeros((2, r_m, tq), jnp.float32)
    st_ref[:, r_m:r_l, :] = jnp.full((2, SUBLANES, tq), -jnp.inf, jnp.float32)
    st_ref[:, r_l:, :] = jnp.zeros((2, SUBLANES, tq), jnp.float32)

    chains = [(mp, slice(s_i * tk, (s_i + 1) * tk)) for mp in range(2) for s_i in range(strips)]

    def scores(c):
        k = k_ref[pl.ds(pl.multiple_of(c * tk, tk), tk), :]
        return jnp.stack([lax.dot_general(k, qm_ref[mp, qs, :], _NT, preferred_element_type=jnp.float32)
                          for mp, qs in chains])

    sc_ref[0] = scores(0)

    def body(c, _):
        nxt = scores(jnp.minimum(c + 1, nkv - 1))
        cur = sc_ref[c % 2]
        vt = vt_ref[0, c]
        tile = jnp.clip(c - i * strips, -2, strips + 1) + 2
        st = st_ref[...]
        stats, probs = [], []
        for n, (mp, qs) in enumerate(chains):
            s = cur[n] + bias_ref[0, tile, :, qs]
            m_prev = st[mp, r_m:r_l, qs]
            m_new = jnp.maximum(m_prev, jnp.max(s, axis=0, keepdims=True))
            stats.append((jnp.exp2(m_prev - m_new), m_new))
            probs.append(jnp.exp2(s - m_new[:1]).astype(vt.dtype))
        blocks = []
        for (mp, qs), p, (alpha, m_new) in zip(chains, probs, stats):
            pv = jnp.dot(vt, p, preferred_element_type=jnp.float32)
            a_new = alpha[:1] * st[mp, :r_m, qs] + pv[:r_m]
            l_new = alpha * st[mp, r_l:, qs] + pv[r_m:r_l]
            blocks.append(jnp.concatenate([a_new, m_new, l_new], axis=0))
        st_ref[...] = jnp.stack([jnp.concatenate(blocks[mp * strips:(mp + 1) * strips], axis=1)
                                 for mp in range(2)])
        sc_ref[(c + 1) % 2] = nxt
        return 0

    lax.fori_loop(0, nkv, body, 0, unroll=_tile(nkv, KV_GROUP))

    l1, l2 = st_ref[0, r_l:r_l + 1, :], st_ref[1, r_l:r_l + 1, :]
    o = st_ref[0, :r_m, :] / l1 - lam_ref[0] * (st_ref[1, :r_m, :] / l2)
    ms = jnp.mean(o * o, axis=0, keepdims=True)
    o = (o * lax.rsqrt(ms + EPS)).T * (g_ref[...] * post_scale)
    o_ref[...] = o.astype(o_ref.dtype)


def _diff_attention(zqkv, vt, row0, batch, seq, n_heads, lam, bias_tiles, subln_g, post_scale):
    _, n_tiles, tk, tq = bias_tiles.shape
    nt = seq // tq
    rb0 = row0 // tq
    kern = functools.partial(_attn_kernel, tq=tq, tk=tk, post_scale=post_scale)
    return pl.pallas_call(
        kern,
        grid=(batch * n_heads, nt),
        in_specs=[
            pl.BlockSpec(memory_space=pltpu.SMEM),
            pl.BlockSpec((tq, V_DIM), lambda bh, i: (rb0 + (bh // n_heads) * nt + i, bh % n_heads)),
            pl.BlockSpec((seq, V_DIM), lambda bh, i: (row0 // seq + bh // n_heads, n_heads + bh % n_heads)),
            pl.BlockSpec((1, seq // tk, vt.shape[2], tk),
                         lambda bh, i: (bh % n_heads, row0 // seq + bh // n_heads, 0, 0)),
            pl.BlockSpec((1, n_tiles, tk, tq), lambda bh, i: (bh % n_heads, 0, 0, 0)),
            pl.BlockSpec((1, V_DIM), lambda bh, i: (0, 0)),
        ],
        out_specs=pl.BlockSpec((tq, V_DIM), lambda bh, i: ((bh // n_heads) * nt + i, bh % n_heads)),
        out_shape=jax.ShapeDtypeStruct((batch * seq, n_heads * V_DIM), jnp.bfloat16),
        scratch_shapes=[pltpu.VMEM((2, tq, V_DIM), jnp.bfloat16),
                        pltpu.VMEM((2, V_DIM + 2 * SUBLANES, tq), jnp.float32),
                        pltpu.VMEM((2, 2 * tq // tk, tk, tk), jnp.float32)],
        compiler_params=_params("parallel", "arbitrary"),
        name="diff_attention",
    )(lam.reshape(1), zqkv, zqkv, vt, bias_tiles, subln_g.reshape(1, V_DIM))


def _outproj_kernel(x_ref, u_ref, a_ref, wu_ref, wa_ref, g_ref, wr_ref, x1_ref, h_ref, aff_ref, *, n_experts):
    y = jnp.dot(u_ref[...], wu_ref[...], preferred_element_type=jnp.float32)
    y = y + jnp.dot(a_ref[...], wa_ref[...], preferred_element_type=jnp.float32)
    x1 = x_ref[...] + y
    x1_ref[...] = x1
    ms = jnp.mean(x1 * x1, axis=-1, keepdims=True)
    h = (x1 * lax.rsqrt(ms + EPS) * g_ref[...]).astype(h_ref.dtype)
    h_ref[...] = h
    logits = jnp.dot(h, wr_ref[...], preferred_element_type=jnp.float32)
    lane = lax.broadcasted_iota(jnp.int32, logits.shape, 1)
    logits = jnp.where(lane < n_experts, logits, -jnp.inf)
    e = jnp.exp(logits - jnp.max(logits, axis=-1, keepdims=True))
    aff_ref[...] = e / jnp.sum(e, axis=-1, keepdims=True)


def _outproj_router(x, u, a, w_u, w_a, g, w_r, n_experts):
    t, d = x.shape
    cw, aw = u.shape[1], a.shape[1]
    tm = _tile(t, 256)
    kern = functools.partial(_outproj_kernel, n_experts=n_experts)
    return pl.pallas_call(
        kern,
        grid=(t // tm,),
        in_specs=[
            pl.BlockSpec((tm, d), lambda i: (i, 0)),
            pl.BlockSpec((tm, cw), lambda i: (i, 0)),
            pl.BlockSpec((tm, aw), lambda i: (i, 0)),
            pl.BlockSpec((cw, d), lambda i: (0, 0)),
            pl.BlockSpec((aw, d), lambda i: (0, 0)),
            pl.BlockSpec((1, d), lambda i: (0, 0)),
            pl.BlockSpec((d, LANES), lambda i: (0, 0)),
        ],
        out_specs=[
            pl.BlockSpec((tm, d), lambda i: (i, 0)),
            pl.BlockSpec((tm, d), lambda i: (i, 0)),
            pl.BlockSpec((tm, LANES), lambda i: (i, 0)),
        ],
        out_shape=[
            jax.ShapeDtypeStruct((t, d), jnp.float32),
            jax.ShapeDtypeStruct((t, d), jnp.bfloat16),
            jax.ShapeDtypeStruct((t, LANES), jnp.float32),
        ],
        compiler_params=_params("parallel"),
        name="outproj_router",
    )(x, u, a, w_u, w_a, g.reshape(1, d), w_r)


def _gate_up_kernel(x_ref, wg_ref, wu_ref, o_ref):
    x = x_ref[0]
    g = jnp.dot(x, wg_ref[0, 0].astype(x.dtype), preferred_element_type=jnp.float32)
    u = jnp.dot(x, wu_ref[0, 0].astype(x.dtype), preferred_element_type=jnp.float32)
    o_ref[0] = (g * jax.nn.sigmoid(g) * u).astype(o_ref.dtype)


def _expert_gate_up(xe, w_gate, w_up, layer, e0):
    e, m, d = xe.shape
    f = w_gate.shape[3]
    tm, tn = _tile(m, 1024), _tile(f, 512)
    return pl.pallas_call(
        _gate_up_kernel,
        grid=(e, f // tn, m // tm),
        in_specs=[
            pl.BlockSpec((1, tm, d), lambda ei, n, mi: (ei, mi, 0)),
            pl.BlockSpec((1, 1, d, tn), lambda ei, n, mi: (layer, e0 + ei, 0, n)),
            pl.BlockSpec((1, 1, d, tn), lambda ei, n, mi: (layer, e0 + ei, 0, n)),
        ],
        out_specs=pl.BlockSpec((1, tm, tn), lambda ei, n, mi: (ei, mi, n)),
        out_shape=jax.ShapeDtypeStruct((e, m, f), jnp.bfloat16),
        compiler_params=_params("parallel", "parallel", "arbitrary"),
        name="expert_gate_up",
    )(xe, w_gate, w_up)


def _down_kernel(h_ref, w_ref, gate_ref, o_ref):
    h = h_ref[0]
    y = jnp.dot(h, w_ref[0, 0].astype(h.dtype), preferred_element_type=jnp.float32)
    reps = y.shape[1] // LANES
    o_ref[0] = y * jnp.concatenate([gate_ref[0]] * reps, axis=1)


def _expert_down(h, w_down, gates_b, layer, e0):
    e, m, f = h.shape
    d = w_down.shape[3]
    tm, tn = _tile(m, 1024), _tile(d, 256)
    return pl.pallas_call(
        _down_kernel,
        grid=(e, m // tm, d // tn),
        in_specs=[
            pl.BlockSpec((1, tm, f), lambda ei, mi, n: (ei, mi, 0)),
            pl.BlockSpec((1, 1, f, tn), lambda ei, mi, n: (layer, e0 + ei, 0, n)),
            pl.BlockSpec((1, tm, LANES), lambda ei, mi, n: (ei, mi, 0)),
        ],
        out_specs=pl.BlockSpec((1, tm, tn), lambda ei, mi, n: (ei, mi, n)),
        out_shape=jax.ShapeDtypeStruct((e, m, d), jnp.float32),
        compiler_params=_params("parallel", "parallel", "arbitrary"),
        name="expert_down",
    )(h, w_down, gates_b)


def _rmsnorm_kernel(x_ref, g_ref, o_ref):
    x = x_ref[...]
    ms = jnp.mean(x * x, axis=-1, keepdims=True)
    o_ref[...] = x * lax.rsqrt(ms + EPS) * g_ref[...]


def _rmsnorm(x, g):
    t, d = x.shape
    tm = _tile(t, 512)
    return pl.pallas_call(
        _rmsnorm_kernel,
        grid=(t // tm,),
        in_specs=[pl.BlockSpec((tm, d), lambda i: (i, 0)), pl.BlockSpec((1, d), lambda i: (0, 0))],
        out_specs=pl.BlockSpec((tm, d), lambda i: (i, 0)),
        out_shape=jax.ShapeDtypeStruct((t, d), jnp.float32),
        compiler_params=_params("parallel"),
        name="final_rmsnorm",
    )(x, g.reshape(1, d))


def _route(aff, n_experts):
    n = aff.shape[0]
    cap = CAPACITY_FACTOR * n // n_experts
    return lax.top_k(aff[:, :n_experts].T, cap)


def kernel(x_prompt, x_sample, rel_bias, final_g, ln1_g, w_in, conv_w, conv_b, conv_ln_g, conv_ln_b,
           lam_q1, lam_k1, lam_q2, lam_k2, subln_g, w_out, ln2_g, w_router, w_gate, w_up, w_down):
    bp, sp, d = x_prompt.shape
    bs, ss, _ = x_sample.shape
    n_p, n_s = bp * sp, bs * ss
    depth = w_in.shape[0]
    cw = conv_w.shape[2]
    n_heads = (w_in.shape[2] - 2 * cw) // (3 * V_DIM)
    n_experts = w_router.shape[2]
    bf16 = jnp.bfloat16

    x = jnp.concatenate([x_prompt.reshape(n_p, d), x_sample.reshape(n_s, d)], axis=0)
    tq_attn = _tile(math.gcd(sp, ss), 512)
    tk_attn = _tile(tq_attn, 256)
    bias_tiles = _bias_tiles(rel_bias, tq_attn, tk_attn)
    qk_w = n_heads * V_DIM
    qkv_scale = jnp.concatenate([jnp.full((qk_w,), LOG2E * HEAD_DIM ** -0.5, jnp.float32),
                                 jnp.ones((2 * qk_w,), jnp.float32)])

    for l in range(depth):
        w_in_l = w_in[l].astype(bf16)
        zc = _norm_matmul(x, ln1_g[l], w_in_l[:, :2 * cw], jnp.ones((2 * cw,), jnp.float32), jnp.float32)
        zqkv = _norm_matmul(x, ln1_g[l], w_in_l[:, 2 * cw:], qkv_scale, bf16)

        u = _conv_module(zc, conv_w[l], conv_b[l], conv_ln_g[l], conv_ln_b[l], n_p, sp, ss)

        lam_init = 0.8 - 0.6 * math.exp(-0.3 * l)
        lam = (jnp.exp(jnp.sum(lam_q1[l] * lam_k1[l])) - jnp.exp(jnp.sum(lam_q2[l] * lam_k2[l])) + lam_init)
        vt = zqkv[:, 2 * qk_w:].reshape((n_p + n_s) // tk_attn, tk_attn, n_heads, V_DIM).transpose(2, 0, 3, 1)
        vt = jnp.concatenate([vt, jnp.ones(vt.shape[:2] + (ONES_ROWS, tk_attn), bf16)], axis=2)
        attn = functools.partial(_diff_attention, zqkv, vt, n_heads=n_heads, lam=lam, bias_tiles=bias_tiles,
                                 subln_g=subln_g[l], post_scale=1.0 - lam_init)
        a = jnp.concatenate([attn(row0=0, batch=bp, seq=sp), attn(row0=n_p, batch=bs, seq=ss)], axis=0)

        w_out_l = w_out[l].astype(bf16)
        w_r = jnp.pad(w_router[l], ((0, 0), (0, LANES - n_experts))).astype(bf16)
        x1, h, aff = _outproj_router(x, u, a, w_out_l[:cw], w_out_l[cw:], ln2_g[l], w_r, n_experts)

        gates_p, idx_p = _route(aff[:n_p], n_experts)
        gates_s, idx_s = _route(aff[n_p:], n_experts)
        gates = jnp.concatenate([gates_p, gates_s], axis=1)
        idx = jnp.concatenate([idx_p, idx_s + n_p], axis=1)

        x = x1
        eg = _tile(n_experts, EXPERT_GROUP)
        for e0 in range(0, n_experts, eg):
            idx_g = idx[e0:e0 + eg]
            hid = _expert_gate_up(h[idx_g], w_gate, w_up, l, e0)
            gates_b = jnp.broadcast_to(gates[e0:e0 + eg, :, None], idx_g.shape + (LANES,))
            ye = _expert_down(hid, w_down, gates_b, l, e0)
            x = x.at[idx_g.reshape(-1)].add(ye.reshape(-1, d))

    y = _rmsnorm(x, final_g)
    return y[:n_p].reshape(bp, sp, d), y[n_p:].reshape(bs, ss, d)
```

```python
import functools
import math

import jax
import jax.numpy as jnp
from jax import lax
from jax.experimental import pallas as pl
from jax.experimental.pallas import tpu as pltpu

EPS = 1e-6
CONV_KERNEL = 31
CONV_PAD = CONV_KERNEL // 2
HALO = 16
HEAD_DIM = 64
V_DIM = 2 * HEAD_DIM
N_BUCKETS = 32
MAX_DISTANCE = 128
CAPACITY_FACTOR = 2
LANES = 128
SUBLANES = 8
EXPERT_GROUP = 16
KV_GROUP = 8
ONES_ROWS = 16
LOG2E = math.log2(math.e)
VMEM_LIMIT_BYTES = 56 * 1024 * 1024

_NT = (((1,), (1,)), ((), ()))


def _tile(n, pref):
    t = min(n, pref)
    while n % t:
        t -= 1
    return t


def _params(*sem):
    return pltpu.CompilerParams(dimension_semantics=sem, vmem_limit_bytes=VMEM_LIMIT_BYTES)


def _norm_matmul_kernel(x_ref, g_ref, w_ref, s_ref, o_ref, h_ref):
    @pl.when(pl.program_id(1) == 0)
    def _():
        x = x_ref[...]
        ms = jnp.mean(x * x, axis=-1, keepdims=True)
        h_ref[...] = (x * lax.rsqrt(ms + EPS) * g_ref[...]).astype(h_ref.dtype)

    y = jnp.dot(h_ref[...], w_ref[...], preferred_element_type=jnp.float32)
    o_ref[...] = (y * s_ref[...]).astype(o_ref.dtype)


def _norm_matmul(x, g, w, col_scale, out_dtype):
    t, d = x.shape
    n = w.shape[1]
    tm, tn = _tile(t, 1024), _tile(n, 1024)
    return pl.pallas_call(
        _norm_matmul_kernel,
        grid=(t // tm, n // tn),
        in_specs=[
            pl.BlockSpec((tm, d), lambda i, j: (i, 0)),
            pl.BlockSpec((1, d), lambda i, j: (0, 0)),
            pl.BlockSpec((d, tn), lambda i, j: (0, j)),
            pl.BlockSpec((1, tn), lambda i, j: (0, j)),
        ],
        out_specs=pl.BlockSpec((tm, tn), lambda i, j: (i, j)),
        out_shape=jax.ShapeDtypeStruct((t, n), out_dtype),
        scratch_shapes=[pltpu.VMEM((tm, d), jnp.bfloat16)],
        compiler_params=_params("parallel", "arbitrary"),
        name="norm_matmul",
    )(x, g.reshape(1, d), w, col_scale.reshape(1, n))


def _conv_kernel(prev_ref, cur_ref, next_ref, w_ref, b_ref, lg_ref, lb_ref, o_ref, e_ref, es_ref, c_ref,
                 *, ts, cw, n_p, s_p, s_s):
    r0 = pl.program_id(0) * ts
    in_p = r0 < n_p
    local = jnp.where(in_p, r0 % s_p, (r0 - n_p) % s_s)
    slen = jnp.where(in_p, s_p, s_s)
    keep_prev = (local != 0).astype(jnp.float32)
    keep_next = (local + ts != slen).astype(jnp.float32)

    def glu(z):
        return z[:, :cw] * jax.nn.sigmoid(z[:, cw:])

    e_ref[0:HALO, :] = glu(prev_ref[...]) * keep_prev
    e_ref[HALO:HALO + ts, :] = glu(cur_ref[...])
    e_ref[HALO + ts:, :] = glu(next_ref[...]) * keep_next

    off = HALO - CONV_PAD
    rows = es_ref.shape[1]
    for r in range(1, SUBLANES):
        es_ref[r - 1] = e_ref[r:r + rows, :]
    for c in range(cw // LANES):
        cs = slice(c * LANES, (c + 1) * LANES)
        acc = jnp.broadcast_to(b_ref[:, cs], (ts, LANES))
        for r in range(SUBLANES):
            x = e_ref[0:rows, cs] if r == 0 else es_ref[r - 1, :, cs]
            for base in range(0, rows - ts + 1, SUBLANES):
                j = base + r - off
                if 0 <= j < CONV_KERNEL:
                    acc = acc + w_ref[j:j + 1, cs] * x[base:base + ts]
        c_ref[:, cs] = acc

    u = c_ref[...]
    mu = jnp.mean(u, axis=-1, keepdims=True)
    var = jnp.mean(jnp.square(u - mu), axis=-1, keepdims=True)
    y = (u - mu) * lax.rsqrt(var + EPS) * lg_ref[...] + lb_ref[...]
    o_ref[...] = (y * jax.nn.sigmoid(y)).astype(o_ref.dtype)


def _conv_module(zc, conv_w, conv_b, ln_g, ln_b, n_p, s_p, s_s):
    t = zc.shape[0]
    cw = zc.shape[1] // 2
    ts = _tile(math.gcd(s_p, s_s), 256)
    hb = ts // HALO
    last_hb = t // HALO - 1
    shifted_extra = (HALO - CONV_PAD + CONV_KERNEL - 1) // SUBLANES * SUBLANES
    kern = functools.partial(_conv_kernel, ts=ts, cw=cw, n_p=n_p, s_p=s_p, s_s=s_s)
    return pl.pallas_call(
        kern,
        grid=(t // ts,),
        in_specs=[
            pl.BlockSpec((HALO, 2 * cw), lambda i: (jnp.maximum(i * hb - 1, 0), 0)),
            pl.BlockSpec((ts, 2 * cw), lambda i: (i, 0)),
            pl.BlockSpec((HALO, 2 * cw), lambda i: (jnp.minimum((i + 1) * hb, last_hb), 0)),
            pl.BlockSpec((CONV_KERNEL, cw), lambda i: (0, 0)),
            pl.BlockSpec((1, cw), lambda i: (0, 0)),
            pl.BlockSpec((1, cw), lambda i: (0, 0)),
            pl.BlockSpec((1, cw), lambda i: (0, 0)),
        ],
        out_specs=pl.BlockSpec((ts, cw), lambda i: (i, 0)),
        out_shape=jax.ShapeDtypeStruct((t, cw), jnp.bfloat16),
        scratch_shapes=[pltpu.VMEM((ts + 2 * HALO, cw), jnp.float32),
                        pltpu.VMEM((SUBLANES - 1, ts + shifted_extra, cw), jnp.float32),
                        pltpu.VMEM((ts, cw), jnp.float32)],
        compiler_params=_params("parallel"),
        name="conv_module",
    )(zc, zc, zc, conv_w, conv_b.reshape(1, cw), ln_g.reshape(1, cw), ln_b.reshape(1, cw))


def _rel_bucket(rel):
    half = N_BUCKETS // 2
    max_exact = half // 2
    ret = (rel > 0).astype(jnp.int32) * half
    n = jnp.abs(rel)
    nf = jnp.maximum(n, 1).astype(jnp.float32)
    large = max_exact + (jnp.log(nf / max_exact) / math.log(MAX_DISTANCE / max_exact)
                         * (half - max_exact)).astype(jnp.int32)
    large = jnp.minimum(large, half - 1)
    return ret + jnp.where(n < max_exact, n, large)


def _bias_tiles(rel_bias, tq, tk):
    assert tk > MAX_DISTANCE and tq % tk == 0
    d = jnp.arange(-2, tq // tk + 2, dtype=jnp.int32)[:, None, None]
    kk = jnp.arange(tk, dtype=jnp.int32)[None, :, None]
    qq = jnp.arange(tq, dtype=jnp.int32)[None, None, :]
    buckets = _rel_bucket(d * tk + kk - qq)[None]
    table = rel_bias.astype(jnp.float32).T * LOG2E
    out = jnp.zeros((table.shape[0],) + buckets.shape[1:], jnp.float32)
    for bkt in range(N_BUCKETS):
        out = jnp.where(buckets == bkt, table[:, bkt][:, None, None, None], out)
    return out


def _attn_kernel(lam_ref, q_ref, k_ref, vt_ref, bias_ref, g_ref, o_ref, qm_ref, st_ref, sc_ref,
                 *, tq, tk, post_scale):
    i = pl.program_id(1)
    nkv = k_ref.shape[0] // tk
    strips = tq // tk
    q = q_ref[...]
    lane = lax.broadcasted_iota(jnp.int32, q.shape, 1)
    zero = jnp.zeros_like(q)
    qm_ref[0] = jnp.where(lane < HEAD_DIM, q, zero)
    qm_ref[1] = jnp.where(lane >= HEAD_DIM, q, zero)
    r_m, r_l = V_DIM, V_DIM + SUBLANES
    st_ref[:, :r_m, :] = jnp.zeros((2, r_m, tq), jnp.float32)
    st_ref[:, r_m:r_l, :] = jnp.full((2, SUBLANES, tq), -jnp.inf, jnp.float32)
    st_ref[:, r_l:, :] = jnp.zeros((2, SUBLANES, tq), jnp.float32)

    chains = [(mp, slice(s_i * tk, (s_i + 1) * tk)) for mp in range(2) for s_i in range(strips)]

    def scores(c):
        k = k_ref[pl.ds(pl.multiple_of(c * tk, tk), tk), :]
        return jnp.stack([lax.dot_general(k, qm_ref[mp, qs, :], _NT, preferred_element_type=jnp.float32)
                          for mp, qs in chains])

    sc_ref[0] = scores(0)

    def body(c, _):
        nxt = scores(jnp.minimum(c + 1, nkv - 1))
        cur = sc_ref[c % 2]
        vt = vt_ref[0, c]
        tile = jnp.clip(c - i * strips, -2, strips + 1) + 2
        st = st_ref[...]
        stats, probs = [], []
        for n, (mp, qs) in enumerate(chains):
            s = cur[n] + bias_ref[0, tile, :, qs]
            m_prev = st[mp, r_m:r_l, qs]
            m_new = jnp.maximum(m_prev, jnp.max(s, axis=0, keepdims=True))
            stats.append((jnp.exp2(m_prev - m_new), m_new))
            probs.append(jnp.exp2(s - m_new[:1]).astype(vt.dtype))
        blocks = []
        for (mp, qs), p, (alpha, m_new) in zip(chains, probs, stats):
            pv = jnp.dot(vt, p, preferred_element_type=jnp.float32)
            a_new = alpha[:1] * st[mp, :r_m, qs] + pv[:r_m]
            l_new = alpha * st[mp, r_l:, qs] + pv[r_m:r_l]
            blocks.append(jnp.concatenate([a_new, m_new, l_new], axis=0))
        st_ref[...] = jnp.stack([jnp.concatenate(blocks[mp * strips:(mp + 1) * strips], axis=1)
                                 for mp in range(2)])
        sc_ref[(c + 1) % 2] = nxt
        return 0

    lax.fori_loop(0, nkv, body, 0, unroll=_tile(nkv, KV_GROUP))

    l1, l2 = st_ref[0, r_l:r_l + 1, :], st_ref[1, r_l:r_l + 1, :]
    o = st_ref[0, :r_m, :] / l1 - lam_ref[0] * (st_ref[1, :r_m, :] / l2)
    ms = jnp.mean(o * o, axis=0, keepdims=True)
    o = (o * lax.rsqrt(ms + EPS)).T * (g_ref[...] * post_scale)
    o_ref[...] = o.astype(o_ref.dtype)


def _diff_attention(zqkv, vt, row0, batch, seq, n_heads, lam, bias_tiles, subln_g, post_scale):
    _, n_tiles, tk, tq = bias_tiles.shape
    nt = seq // tq
    rb0 = row0 // tq
    kern = functools.partial(_attn_kernel, tq=tq, tk=tk, post_scale=post_scale)
    return pl.pallas_call(
        kern,
        grid=(batch * n_heads, nt),
        in_specs=[
            pl.BlockSpec(memory_space=pltpu.SMEM),
            pl.BlockSpec((tq, V_DIM), lambda bh, i: (rb0 + (bh // n_heads) * nt + i, bh % n_heads)),
            pl.BlockSpec((seq, V_DIM), lambda bh, i: (row0 // seq + bh // n_heads, n_heads + bh % n_heads)),
            pl.BlockSpec((1, seq // tk, vt.shape[2], tk),
                         lambda bh, i: (bh % n_heads, row0 // seq + bh // n_heads, 0, 0)),
            pl.BlockSpec((1, n_tiles, tk, tq), lambda bh, i: (bh % n_heads, 0, 0, 0)),
            pl.BlockSpec((1, V_DIM), lambda bh, i: (0, 0)),
        ],
        out_specs=pl.BlockSpec((tq, V_DIM), lambda bh, i: ((bh // n_heads) * nt + i, bh % n_heads)),
        out_shape=jax.ShapeDtypeStruct((batch * seq, n_heads * V_DIM), jnp.bfloat16),
        scratch_shapes=[pltpu.VMEM((2, tq, V_DIM), jnp.bfloat16),
                        pltpu.VMEM((2, V_DIM + 2 * SUBLANES, tq), jnp.float32),
                        pltpu.VMEM((2, 2 * tq // tk, tk, tk), jnp.float32)],
        compiler_params=_params("parallel", "arbitrary"),
        name="diff_attention",
    )(lam.reshape(1), zqkv, zqkv, vt, bias_tiles, subln_g.reshape(1, V_DIM))


def _outproj_kernel(x_ref, u_ref, a_ref, wu_ref, wa_ref, g_ref, wr_ref, x1_ref, h_ref, aff_ref, *, n_experts):
    y = jnp.dot(u_ref[...], wu_ref[...], preferred_element_type=jnp.float32)
    y = y + jnp.dot(a_ref[...], wa_ref[...], preferred_element_type=jnp.float32)
    x1 = x_ref[...] + y
    x1_ref[...] = x1
    ms = jnp.mean(x1 * x1, axis=-1, keepdims=True)
    h = (x1 * lax.rsqrt(ms + EPS) * g_ref[...]).astype(h_ref.dtype)
    h_ref[...] = h
    logits = jnp.dot(h, wr_ref[...], preferred_element_type=jnp.float32)
    lane = lax.broadcasted_iota(jnp.int32, logits.shape, 1)
    logits = jnp.where(lane < n_experts, logits, -jnp.inf)
    e = jnp.exp(logits - jnp.max(logits, axis=-1, keepdims=True))
    aff_ref[...] = e / jnp.sum(e, axis=-1, keepdims=True)


def _outproj_router(x, u, a, w_u, w_a, g, w_r, n_experts):
    t, d = x.shape
    cw, aw = u.shape[1], a.shape[1]
    tm = _tile(t, 256)
    kern = functools.partial(_outproj_kernel, n_experts=n_experts)
    return pl.pallas_call(
        kern,
        grid=(t // tm,),
        in_specs=[
            pl.BlockSpec((tm, d), lambda i: (i, 0)),
            pl.BlockSpec((tm, cw), lambda i: (i, 0)),
            pl.BlockSpec((tm, aw), lambda i: (i, 0)),
            pl.BlockSpec((cw, d), lambda i: (0, 0)),
            pl.BlockSpec((aw, d), lambda i: (0, 0)),
            pl.BlockSpec((1, d), lambda i: (0, 0)),
            pl.BlockSpec((d, LANES), lambda i: (0, 0)),
        ],
        out_specs=[
            pl.BlockSpec((tm, d), lambda i: (i, 0)),
            pl.BlockSpec((tm, d), lambda i: (i, 0)),
            pl.BlockSpec((tm, LANES), lambda i: (i, 0)),
        ],
        out_shape=[
            jax.ShapeDtypeStruct((t, d), jnp.float32),
            jax.ShapeDtypeStruct((t, d), jnp.bfloat16),
            jax.ShapeDtypeStruct((t, LANES), jnp.float32),
        ],
        compiler_params=_params("parallel"),
        name="outproj_router",
    )(x, u, a, w_u, w_a, g.reshape(1, d), w_r)


def _gate_up_kernel(x_ref, wg_ref, wu_ref, o_ref):
    x = x_ref[0]
    g = jnp.dot(x, wg_ref[0, 0].astype(x.dtype), preferred_element_type=jnp.float32)
    u = jnp.dot(x, wu_ref[0, 0].astype(x.dtype), preferred_element_type=jnp.float32)
    o_ref[0] = (g * jax.nn.sigmoid(g) * u).astype(o_ref.dtype)


def _expert_gate_up(xe, w_gate, w_up, layer, e0):
    e, m, d = xe.shape
    f = w_gate.shape[3]
    tm, tn = _tile(m, 1024), _tile(f, 512)
    return pl.pallas_call(
        _gate_up_kernel,
        grid=(e, f // tn, m // tm),
        in_specs=[
            pl.BlockSpec((1, tm, d), lambda ei, n, mi: (ei, mi, 0)),
            pl.BlockSpec((1, 1, d, tn), lambda ei, n, mi: (layer, e0 + ei, 0, n)),
            pl.BlockSpec((1, 1, d, tn), lambda ei, n, mi: (layer, e0 + ei, 0, n)),
        ],
        out_specs=pl.BlockSpec((1, tm, tn), lambda ei, n, mi: (ei, mi, n)),
        out_shape=jax.ShapeDtypeStruct((e, m, f), jnp.bfloat16),
        compiler_params=_params("parallel", "parallel", "arbitrary"),
        name="expert_gate_up",
    )(xe, w_gate, w_up)


def _down_kernel(h_ref, w_ref, gate_ref, o_ref):
    h = h_ref[0]
    y = jnp.dot(h, w_ref[0, 0].astype(h.dtype), preferred_element_type=jnp.float32)
    reps = y.shape[1] // LANES
    o_ref[0] = y * jnp.concatenate([gate_ref[0]] * reps, axis=1)


def _expert_down(h, w_down, gates_b, layer, e0):
    e, m, f = h.shape
    d = w_down.shape[3]
    tm, tn = _tile(m, 1024), _tile(d, 256)
    return pl.pallas_call(
        _down_kernel,
        grid=(e, m // tm, d // tn),
        in_specs=[
            pl.BlockSpec((1, tm, f), lambda ei, mi, n: (ei, mi, 0)),
            pl.BlockSpec((1, 1, f, tn), lambda ei, mi, n: (layer, e0 + ei, 0, n)),
            pl.BlockSpec((1, tm, LANES), lambda ei, mi, n: (ei, mi, 0)),
        ],
        out_specs=pl.BlockSpec((1, tm, tn), lambda ei, mi, n: (ei, mi, n)),
        out_shape=jax.ShapeDtypeStruct((e, m, d), jnp.float32),
        compiler_params=_params("parallel", "parallel", "arbitrary"),
        name="expert_down",
    )(h, w_down, gates_b)


def _rmsnorm_kernel(x_ref, g_ref, o_ref):
    x = x_ref[...]
    ms = jnp.mean(x * x, axis=-1, keepdims=True)
    o_ref[...] = x * lax.rsqrt(ms + EPS) * g_ref[...]


def _rmsnorm(x, g):
    t, d = x.shape
    tm = _tile(t, 512)
    return pl.pallas_call(
        _rmsnorm_kernel,
        grid=(t // tm,),
        in_specs=[pl.BlockSpec((tm, d), lambda i: (i, 0)), pl.BlockSpec((1, d), lambda i: (0, 0))],
        out_specs=pl.BlockSpec((tm, d), lambda i: (i, 0)),
        out_shape=jax.ShapeDtypeStruct((t, d), jnp.float32),
        compiler_params=_params("parallel"),
        name="final_rmsnorm",
    )(x, g.reshape(1, d))


def _route(aff, n_experts):
    n = aff.shape[0]
    cap = CAPACITY_FACTOR * n // n_experts
    gates, idx = lax.top_k(aff[:, :n_experts].T, cap)
    idx, gates = lax.sort((idx, gates), dimension=1, num_keys=1)
    return gates, idx


def kernel(x_prompt, x_sample, rel_bias, final_g, ln1_g, w_in, conv_w, conv_b, conv_ln_g, conv_ln_b,
           lam_q1, lam_k1, lam_q2, lam_k2, subln_g, w_out, ln2_g, w_router, w_gate, w_up, w_down):
    bp, sp, d = x_prompt.shape
    bs, ss, _ = x_sample.shape
    n_p, n_s = bp * sp, bs * ss
    depth = w_in.shape[0]
    cw = conv_w.shape[2]
    n_heads = (w_in.shape[2] - 2 * cw) // (3 * V_DIM)
    n_experts = w_router.shape[2]
    bf16 = jnp.bfloat16

    x = jnp.concatenate([x_prompt.reshape(n_p, d), x_sample.reshape(n_s, d)], axis=0)
    tq_attn = _tile(math.gcd(sp, ss), 512)
    tk_attn = _tile(tq_attn, 256)
    bias_tiles = _bias_tiles(rel_bias, tq_attn, tk_attn)
    qk_w = n_heads * V_DIM
    qkv_scale = jnp.concatenate([jnp.full((qk_w,), LOG2E * HEAD_DIM ** -0.5, jnp.float32),
                                 jnp.ones((2 * qk_w,), jnp.float32)])

    for l in range(depth):
        w_in_l = w_in[l].astype(bf16)
        zc = _norm_matmul(x, ln1_g[l], w_in_l[:, :2 * cw], jnp.ones((2 * cw,), jnp.float32), jnp.float32)
        zqkv = _norm_matmul(x, ln1_g[l], w_in_l[:, 2 * cw:], qkv_scale, bf16)

        u = _conv_module(zc, conv_w[l], conv_b[l], conv_ln_g[l], conv_ln_b[l], n_p, sp, ss)

        lam_init = 0.8 - 0.6 * math.exp(-0.3 * l)
        lam = (jnp.exp(jnp.sum(lam_q1[l] * lam_k1[l])) - jnp.exp(jnp.sum(lam_q2[l] * lam_k2[l])) + lam_init)
        vt = zqkv[:, 2 * qk_w:].reshape((n_p + n_s) // tk_attn, tk_attn, n_heads, V_DIM).transpose(2, 0, 3, 1)
        vt = jnp.concatenate([vt, jnp.ones(vt.shape[:2] + (ONES_ROWS, tk_attn), bf16)], axis=2)
        attn = functools.partial(_diff_attention, zqkv, vt, n_heads=n_heads, lam=lam, bias_tiles=bias_tiles,
                                 subln_g=subln_g[l], post_scale=1.0 - lam_init)
        a = jnp.concatenate([attn(row0=0, batch=bp, seq=sp), attn(row0=n_p, batch=bs, seq=ss)], axis=0)

        w_out_l = w_out[l].astype(bf16)
        w_r = jnp.pad(w_router[l], ((0, 0), (0, LANES - n_experts))).astype(bf16)
        x1, h, aff = _outproj_router(x, u, a, w_out_l[:cw], w_out_l[cw:], ln2_g[l], w_r, n_experts)

        gates_p, idx_p = _route(aff[:n_p], n_experts)
        gates_s, idx_s = _route(aff[n_p:], n_experts)
        gates = jnp.concatenate([gates_p, gates_s], axis=1)
        idx = jnp.concatenate([idx_p, idx_s + n_p], axis=1)

        x = x1
        eg = _tile(n_experts, EXPERT_GROUP)
        for e0 in range(0, n_experts, eg):
            idx_g = idx[e0:e0 + eg]
            hid = _expert_gate_up(h[idx_g], w_gate, w_up, l, e0)
            gates_b = jnp.broadcast_to(gates[e0:e0 + eg, :, None], idx_g.shape + (LANES,))
            ye = _expert_down(hid, w_down, gates_b, l, e0)
            x = x.at[idx_g.reshape(-1)].add(ye.reshape(-1, d))

    y = _rmsnorm(x, final_g)
    return y[:n_p].reshape(bp, sp, d), y[n_p:].reshape(bs, ss, d)
```

```python
import functools
import math

import jax
import jax.numpy as jnp
from jax import lax
from jax.experimental import pallas as pl
from jax.experimental.pallas import tpu as pltpu

EPS = 1e-6
CONV_KERNEL = 31
CONV_PAD = CONV_KERNEL // 2
HALO = 16
HEAD_DIM = 64
V_DIM = 2 * HEAD_DIM
N_BUCKETS = 32
MAX_DISTANCE = 128
CAPACITY_FACTOR = 2
LANES = 128
SUBLANES = 8
EXPERT_GROUP = 16
KV_GROUP = 16
ONES_ROWS = 16
LOG2E = math.log2(math.e)
VMEM_LIMIT_BYTES = 56 * 1024 * 1024

_NT = (((1,), (1,)), ((), ()))


def _tile(n, pref):
    t = min(n, pref)
    while n % t:
        t -= 1
    return t


def _params(*sem):
    return pltpu.CompilerParams(dimension_semantics=sem, vmem_limit_bytes=VMEM_LIMIT_BYTES)


def _norm_matmul_kernel(x_ref, g_ref, w_ref, s_ref, o_ref, h_ref):
    @pl.when(pl.program_id(1) == 0)
    def _():
        x = x_ref[...]
        ms = jnp.mean(x * x, axis=-1, keepdims=True)
        h_ref[...] = (x * lax.rsqrt(ms + EPS) * g_ref[...]).astype(h_ref.dtype)

    y = jnp.dot(h_ref[...], w_ref[...], preferred_element_type=jnp.float32)
    o_ref[...] = (y * s_ref[...]).astype(o_ref.dtype)


def _norm_matmul(x, g, w, col_scale, out_dtype):
    t, d = x.shape
    n = w.shape[1]
    tm, tn = _tile(t, 1024), _tile(n, 1024)
    return pl.pallas_call(
        _norm_matmul_kernel,
        grid=(t // tm, n // tn),
        in_specs=[
            pl.BlockSpec((tm, d), lambda i, j: (i, 0)),
            pl.BlockSpec((1, d), lambda i, j: (0, 0)),
            pl.BlockSpec((d, tn), lambda i, j: (0, j)),
            pl.BlockSpec((1, tn), lambda i, j: (0, j)),
        ],
        out_specs=pl.BlockSpec((tm, tn), lambda i, j: (i, j)),
        out_shape=jax.ShapeDtypeStruct((t, n), out_dtype),
        scratch_shapes=[pltpu.VMEM((tm, d), jnp.bfloat16)],
        compiler_params=_params("parallel", "arbitrary"),
        name="norm_matmul",
    )(x, g.reshape(1, d), w, col_scale.reshape(1, n))


def _conv_kernel(prev_ref, cur_ref, next_ref, w_ref, b_ref, lg_ref, lb_ref, o_ref, e_ref, es_ref, c_ref,
                 *, ts, cw, n_p, s_p, s_s):
    r0 = pl.program_id(0) * ts
    in_p = r0 < n_p
    local = jnp.where(in_p, r0 % s_p, (r0 - n_p) % s_s)
    slen = jnp.where(in_p, s_p, s_s)
    keep_prev = (local != 0).astype(jnp.float32)
    keep_next = (local + ts != slen).astype(jnp.float32)

    def glu(z):
        return z[:, :cw] * jax.nn.sigmoid(z[:, cw:])

    e_ref[0:HALO, :] = glu(prev_ref[...]) * keep_prev
    e_ref[HALO:HALO + ts, :] = glu(cur_ref[...])
    e_ref[HALO + ts:, :] = glu(next_ref[...]) * keep_next

    off = HALO - CONV_PAD
    rows = es_ref.shape[1]
    for r in range(1, SUBLANES):
        es_ref[r - 1] = e_ref[r:r + rows, :]
    for c in range(cw // LANES):
        cs = slice(c * LANES, (c + 1) * LANES)
        acc = jnp.broadcast_to(b_ref[:, cs], (ts, LANES))
        for r in range(SUBLANES):
            x = e_ref[0:rows, cs] if r == 0 else es_ref[r - 1, :, cs]
            for base in range(0, rows - ts + 1, SUBLANES):
                j = base + r - off
                if 0 <= j < CONV_KERNEL:
                    acc = acc + w_ref[j:j + 1, cs] * x[base:base + ts]
        c_ref[:, cs] = acc

    u = c_ref[...]
    mu = jnp.mean(u, axis=-1, keepdims=True)
    var = jnp.mean(jnp.square(u - mu), axis=-1, keepdims=True)
    y = (u - mu) * lax.rsqrt(var + EPS) * lg_ref[...] + lb_ref[...]
    o_ref[...] = (y * jax.nn.sigmoid(y)).astype(o_ref.dtype)


def _conv_module(zc, conv_w, conv_b, ln_g, ln_b, n_p, s_p, s_s):
    t = zc.shape[0]
    cw = zc.shape[1] // 2
    ts = _tile(math.gcd(s_p, s_s), 256)
    hb = ts // HALO
    last_hb = t // HALO - 1
    shifted_extra = (HALO - CONV_PAD + CONV_KERNEL - 1) // SUBLANES * SUBLANES
    kern = functools.partial(_conv_kernel, ts=ts, cw=cw, n_p=n_p, s_p=s_p, s_s=s_s)
    return pl.pallas_call(
        kern,
        grid=(t // ts,),
        in_specs=[
            pl.BlockSpec((HALO, 2 * cw), lambda i: (jnp.maximum(i * hb - 1, 0), 0)),
            pl.BlockSpec((ts, 2 * cw), lambda i: (i, 0)),
            pl.BlockSpec((HALO, 2 * cw), lambda i: (jnp.minimum((i + 1) * hb, last_hb), 0)),
            pl.BlockSpec((CONV_KERNEL, cw), lambda i: (0, 0)),
            pl.BlockSpec((1, cw), lambda i: (0, 0)),
            pl.BlockSpec((1, cw), lambda i: (0, 0)),
            pl.BlockSpec((1, cw), lambda i: (0, 0)),
        ],
        out_specs=pl.BlockSpec((ts, cw), lambda i: (i, 0)),
        out_shape=jax.ShapeDtypeStruct((t, cw), jnp.bfloat16),
        scratch_shapes=[pltpu.VMEM((ts + 2 * HALO, cw), jnp.float32),
                        pltpu.VMEM((SUBLANES - 1, ts + shifted_extra, cw), jnp.float32),
                        pltpu.VMEM((ts, cw), jnp.float32)],
        compiler_params=_params("parallel"),
        name="conv_module",
    )(zc, zc, zc, conv_w, conv_b.reshape(1, cw), ln_g.reshape(1, cw), ln_b.reshape(1, cw))


def _rel_bucket(rel):
    half = N_BUCKETS // 2
    max_exact = half // 2
    ret = (rel > 0).astype(jnp.int32) * half
    n = jnp.abs(rel)
    nf = jnp.maximum(n, 1).astype(jnp.float32)
    large = max_exact + (jnp.log(nf / max_exact) / math.log(MAX_DISTANCE / max_exact)
                         * (half - max_exact)).astype(jnp.int32)
    large = jnp.minimum(large, half - 1)
    return ret + jnp.where(n < max_exact, n, large)


def _bias_tiles(rel_bias, tq, tk):
    assert tk > MAX_DISTANCE and tq % tk == 0
    d = jnp.arange(-2, tq // tk + 2, dtype=jnp.int32)[:, None, None]
    kk = jnp.arange(tk, dtype=jnp.int32)[None, :, None]
    qq = jnp.arange(tq, dtype=jnp.int32)[None, None, :]
    buckets = _rel_bucket(d * tk + kk - qq)[None]
    table = rel_bias.astype(jnp.float32).T * LOG2E
    out = jnp.zeros((table.shape[0],) + buckets.shape[1:], jnp.float32)
    for bkt in range(N_BUCKETS):
        out = jnp.where(buckets == bkt, table[:, bkt][:, None, None, None], out)
    return out


def _attn_kernel(lam_ref, q_ref, k_ref, vt_ref, bias_ref, g_ref, o_ref, qm_ref, st_ref, sc_ref,
                 *, tq, tk, post_scale):
    i = pl.program_id(1)
    nkv = k_ref.shape[0] // tk
    strips = tq // tk
    q = q_ref[...]
    lane = lax.broadcasted_iota(jnp.int32, q.shape, 1)
    zero = jnp.zeros_like(q)
    qm_ref[0] = jnp.where(lane < HEAD_DIM, q, zero)
    qm_ref[1] = jnp.where(lane >= HEAD_DIM, q, zero)
    r_m, r_l = V_DIM, V_DIM + SUBLANES
    st_ref[:, :r_m, :] = jnp.zeros((2, r_m, tq), jnp.float32)
    st_ref[:, r_m:r_l, :] = jnp.full((2, SUBLANES, tq), -jnp.inf, jnp.float32)
    st_ref[:, r_l:, :] = jnp.zeros((2, SUBLANES, tq), jnp.float32)

    chains = [(mp, slice(s_i * tk, (s_i + 1) * tk)) for mp in range(2) for s_i in range(strips)]

    def scores(c):
        k = k_ref[pl.ds(pl.multiple_of(c * tk, tk), tk), :]
        return jnp.stack([lax.dot_general(k, qm_ref[mp, qs, :], _NT, preferred_element_type=jnp.float32)
                          for mp, qs in chains])

    sc_ref[0] = scores(0)

    def body(c, _):
        nxt = scores(jnp.minimum(c + 1, nkv - 1))
        cur = sc_ref[c % 2]
        vt = vt_ref[0, c]
        tile = jnp.clip(c - i * strips, -2, strips + 1) + 2
        st = st_ref[...]
        stats, probs = [], []
        for n, (mp, qs) in enumerate(chains):
            s = cur[n] + bias_ref[0, tile, :, qs]
            m_prev = st[mp, r_m:r_l, qs]
            m_new = jnp.maximum(m_prev, jnp.max(s, axis=0, keepdims=True))
            stats.append((jnp.exp2(m_prev - m_new), m_new))
            probs.append(jnp.exp2(s - m_new[:1]).astype(vt.dtype))
        blocks = []
        for (mp, qs), p, (alpha, m_new) in zip(chains, probs, stats):
            pv = jnp.dot(vt, p, preferred_element_type=jnp.float32)
            a_new = alpha[:1] * st[mp, :r_m, qs] + pv[:r_m]
            l_new = alpha * st[mp, r_l:, qs] + pv[r_m:r_l]
            blocks.append(jnp.concatenate([a_new, m_new, l_new], axis=0))
        st_ref[...] = jnp.stack([jnp.concatenate(blocks[mp * strips:(mp + 1) * strips], axis=1)
                                 for mp in range(2)])
        sc_ref[(c + 1) % 2] = nxt
        return 0

    lax.fori_loop(0, nkv, body, 0, unroll=_tile(nkv, KV_GROUP))

    l1, l2 = st_ref[0, r_l:r_l + 1, :], st_ref[1, r_l:r_l + 1, :]
    o = st_ref[0, :r_m, :] / l1 - lam_ref[0] * (st_ref[1, :r_m, :] / l2)
    ms = jnp.mean(o * o, axis=0, keepdims=True)
    o = (o * lax.rsqrt(ms + EPS)).T * (g_ref[...] * post_scale)
    o_ref[...] = o.astype(o_ref.dtype)


def _diff_attention(zqkv, vt, row0, batch, seq, n_heads, lam, bias_tiles, subln_g, post_scale):
    _, n_tiles, tk, tq = bias_tiles.shape
    nt = seq // tq
    rb0 = row0 // tq
    kern = functools.partial(_attn_kernel, tq=tq, tk=tk, post_scale=post_scale)
    return pl.pallas_call(
        kern,
        grid=(batch * n_heads, nt),
        in_specs=[
            pl.BlockSpec(memory_space=pltpu.SMEM),
            pl.BlockSpec((tq, V_DIM), lambda bh, i: (rb0 + (bh // n_heads) * nt + i, bh % n_heads)),
            pl.BlockSpec((seq, V_DIM), lambda bh, i: (row0 // seq + bh // n_heads, n_heads + bh % n_heads)),
            pl.BlockSpec((1, seq // tk, vt.shape[2], tk),
                         lambda bh, i: (bh % n_heads, row0 // seq + bh // n_heads, 0, 0)),
            pl.BlockSpec((1, n_tiles, tk, tq), lambda bh, i: (bh % n_heads, 0, 0, 0)),
            pl.BlockSpec((1, V_DIM), lambda bh, i: (0, 0)),
        ],
        out_specs=pl.BlockSpec((tq, V_DIM), lambda bh, i: ((bh // n_heads) * nt + i, bh % n_heads)),
        out_shape=jax.ShapeDtypeStruct((batch * seq, n_heads * V_DIM), jnp.bfloat16),
        scratch_shapes=[pltpu.VMEM((2, tq, V_DIM), jnp.bfloat16),
                        pltpu.VMEM((2, V_DIM + 2 * SUBLANES, tq), jnp.float32),
                        pltpu.VMEM((2, 2 * tq // tk, tk, tk), jnp.float32)],
        compiler_params=_params("parallel", "arbitrary"),
        name="diff_attention",
    )(lam.reshape(1), zqkv, zqkv, vt, bias_tiles, subln_g.reshape(1, V_DIM))


def _outproj_kernel(x_ref, u_ref, a_ref, wu_ref, wa_ref, g_ref, wr_ref, x1_ref, h_ref, aff_ref, *, n_experts):
    y = jnp.dot(u_ref[...], wu_ref[...], preferred_element_type=jnp.float32)
    y = y + jnp.dot(a_ref[...], wa_ref[...], preferred_element_type=jnp.float32)
    x1 = x_ref[...] + y
    x1_ref[...] = x1
    ms = jnp.mean(x1 * x1, axis=-1, keepdims=True)
    h = (x1 * lax.rsqrt(ms + EPS) * g_ref[...]).astype(h_ref.dtype)
    h_ref[...] = h
    logits = jnp.dot(h, wr_ref[...], preferred_element_type=jnp.float32)
    lane = lax.broadcasted_iota(jnp.int32, logits.shape, 1)
    logits = jnp.where(lane < n_experts, logits, -jnp.inf)
    e = jnp.exp(logits - jnp.max(logits, axis=-1, keepdims=True))
    aff_ref[...] = e / jnp.sum(e, axis=-1, keepdims=True)


def _outproj_router(x, u, a, w_u, w_a, g, w_r, n_experts):
    t, d = x.shape
    cw, aw = u.shape[1], a.shape[1]
    tm = _tile(t, 256)
    kern = functools.partial(_outproj_kernel, n_experts=n_experts)
    return pl.pallas_call(
        kern,
        grid=(t // tm,),
        in_specs=[
            pl.BlockSpec((tm, d), lambda i: (i, 0)),
            pl.BlockSpec((tm, cw), lambda i: (i, 0)),
            pl.BlockSpec((tm, aw), lambda i: (i, 0)),
            pl.BlockSpec((cw, d), lambda i: (0, 0)),
            pl.BlockSpec((aw, d), lambda i: (0, 0)),
            pl.BlockSpec((1, d), lambda i: (0, 0)),
            pl.BlockSpec((d, LANES), lambda i: (0, 0)),
        ],
        out_specs=[
            pl.BlockSpec((tm, d), lambda i: (i, 0)),
            pl.BlockSpec((tm, d), lambda i: (i, 0)),
            pl.BlockSpec((tm, LANES), lambda i: (i, 0)),
        ],
        out_shape=[
            jax.ShapeDtypeStruct((t, d), jnp.float32),
            jax.ShapeDtypeStruct((t, d), jnp.bfloat16),
            jax.ShapeDtypeStruct((t, LANES), jnp.float32),
        ],
        compiler_params=_params("parallel"),
        name="outproj_router",
    )(x, u, a, w_u, w_a, g.reshape(1, d), w_r)


def _gate_up_kernel(x_ref, wg_ref, wu_ref, o_ref):
    x = x_ref[0]
    g = jnp.dot(x, wg_ref[0, 0].astype(x.dtype), preferred_element_type=jnp.float32)
    u = jnp.dot(x, wu_ref[0, 0].astype(x.dtype), preferred_element_type=jnp.float32)
    o_ref[0] = (g * jax.nn.sigmoid(g) * u).astype(o_ref.dtype)


def _expert_gate_up(xe, w_gate, w_up, layer, e0):
    e, m, d = xe.shape
    f = w_gate.shape[3]
    tm, tn = _tile(m, 1536), _tile(f, 512)
    return pl.pallas_call(
        _gate_up_kernel,
        grid=(e, f // tn, m // tm),
        in_specs=[
            pl.BlockSpec((1, tm, d), lambda ei, n, mi: (ei, mi, 0)),
            pl.BlockSpec((1, 1, d, tn), lambda ei, n, mi: (layer, e0 + ei, 0, n)),
            pl.BlockSpec((1, 1, d, tn), lambda ei, n, mi: (layer, e0 + ei, 0, n)),
        ],
        out_specs=pl.BlockSpec((1, tm, tn), lambda ei, n, mi: (ei, mi, n)),
        out_shape=jax.ShapeDtypeStruct((e, m, f), jnp.bfloat16),
        compiler_params=_params("parallel", "parallel", "arbitrary"),
        name="expert_gate_up",
    )(xe, w_gate, w_up)


def _down_kernel(h_ref, w_ref, gate_ref, o_ref):
    h = h_ref[0]
    y = jnp.dot(h, w_ref[0, 0].astype(h.dtype), preferred_element_type=jnp.float32)
    reps = y.shape[1] // LANES
    o_ref[0] = y * jnp.concatenate([gate_ref[0]] * reps, axis=1)


def _expert_down(h, w_down, gates_b, layer, e0):
    e, m, f = h.shape
    d = w_down.shape[3]
    tm, tn = _tile(m, 1024), _tile(d, 512)
    return pl.pallas_call(
        _down_kernel,
        grid=(e, m // tm, d // tn),
        in_specs=[
            pl.BlockSpec((1, tm, f), lambda ei, mi, n: (ei, mi, 0)),
            pl.BlockSpec((1, 1, f, tn), lambda ei, mi, n: (layer, e0 + ei, 0, n)),
            pl.BlockSpec((1, tm, LANES), lambda ei, mi, n: (ei, mi, 0)),
        ],
        out_specs=pl.BlockSpec((1, tm, tn), lambda ei, mi, n: (ei, mi, n)),
        out_shape=jax.ShapeDtypeStruct((e, m, d), jnp.float32),
        compiler_params=_params("parallel", "parallel", "arbitrary"),
        name="expert_down",
    )(h, w_down, gates_b)


def _rmsnorm_kernel(x_ref, g_ref, o_ref):
    x = x_ref[...]
    ms = jnp.mean(x * x, axis=-1, keepdims=True)
    o_ref[...] = x * lax.rsqrt(ms + EPS) * g_ref[...]


def _rmsnorm(x, g):
    t, d = x.shape
    tm = _tile(t, 512)
    return pl.pallas_call(
        _rmsnorm_kernel,
        grid=(t // tm,),
        in_specs=[pl.BlockSpec((tm, d), lambda i: (i, 0)), pl.BlockSpec((1, d), lambda i: (0, 0))],
        out_specs=pl.BlockSpec((tm, d), lambda i: (i, 0)),
        out_shape=jax.ShapeDtypeStruct((t, d), jnp.float32),
        compiler_params=_params("parallel"),
        name="final_rmsnorm",
    )(x, g.reshape(1, d))


def _route(aff, n_experts):
    n = aff.shape[0]
    cap = CAPACITY_FACTOR * n // n_experts
    gates, idx = lax.top_k(aff[:, :n_experts].T, cap)
    idx, gates = lax.sort((idx, gates), dimension=1, num_keys=1)
    return gates, idx


def kernel(x_prompt, x_sample, rel_bias, final_g, ln1_g, w_in, conv_w, conv_b, conv_ln_g, conv_ln_b,
           lam_q1, lam_k1, lam_q2, lam_k2, subln_g, w_out, ln2_g, w_router, w_gate, w_up, w_down):
    bp, sp, d = x_prompt.shape
    bs, ss, _ = x_sample.shape
    n_p, n_s = bp * sp, bs * ss
    depth = w_in.shape[0]
    cw = conv_w.shape[2]
    n_heads = (w_in.shape[2] - 2 * cw) // (3 * V_DIM)
    n_experts = w_router.shape[2]
    bf16 = jnp.bfloat16

    x = jnp.concatenate([x_prompt.reshape(n_p, d), x_sample.reshape(n_s, d)], axis=0)
    tq_attn = _tile(math.gcd(sp, ss), 512)
    tk_attn = _tile(tq_attn, 256)
    bias_tiles = _bias_tiles(rel_bias, tq_attn, tk_attn)
    qk_w = n_heads * V_DIM
    qkv_scale = jnp.concatenate([jnp.full((qk_w,), LOG2E * HEAD_DIM ** -0.5, jnp.float32),
                                 jnp.ones((2 * qk_w,), jnp.float32)])

    for l in range(depth):
        w_in_l = w_in[l].astype(bf16)
        zc = _norm_matmul(x, ln1_g[l], w_in_l[:, :2 * cw], jnp.ones((2 * cw,), jnp.float32), jnp.float32)
        zqkv = _norm_matmul(x, ln1_g[l], w_in_l[:, 2 * cw:], qkv_scale, bf16)

        u = _conv_module(zc, conv_w[l], conv_b[l], conv_ln_g[l], conv_ln_b[l], n_p, sp, ss)

        lam_init = 0.8 - 0.6 * math.exp(-0.3 * l)
        lam = (jnp.exp(jnp.sum(lam_q1[l] * lam_k1[l])) - jnp.exp(jnp.sum(lam_q2[l] * lam_k2[l])) + lam_init)
        vt = zqkv[:, 2 * qk_w:].reshape((n_p + n_s) // tk_attn, tk_attn, n_heads, V_DIM).transpose(2, 0, 3, 1)
        vt = jnp.concatenate([vt, jnp.ones(vt.shape[:2] + (ONES_ROWS, tk_attn), bf16)], axis=2)
        attn = functools.partial(_diff_attention, zqkv, vt, n_heads=n_heads, lam=lam, bias_tiles=bias_tiles,
                                 subln_g=subln_g[l], post_scale=1.0 - lam_init)
        a = jnp.concatenate([attn(row0=0, batch=bp, seq=sp), attn(row0=n_p, batch=bs, seq=ss)], axis=0)

        w_out_l = w_out[l].astype(bf16)
        w_r = jnp.pad(w_router[l], ((0, 0), (0, LANES - n_experts))).astype(bf16)
        x1, h, aff = _outproj_router(x, u, a, w_out_l[:cw], w_out_l[cw:], ln2_g[l], w_r, n_experts)

        gates_p, idx_p = _route(aff[:n_p], n_experts)
        gates_s, idx_s = _route(aff[n_p:], n_experts)
        gates = jnp.concatenate([gates_p, gates_s], axis=1)
        idx = jnp.concatenate([idx_p, idx_s + n_p], axis=1)

        x = x1
        eg = _tile(n_experts, EXPERT_GROUP)
        for e0 in range(0, n_experts, eg):
            idx_g = idx[e0:e0 + eg]
            hid = _expert_gate_up(h[idx_g], w_gate, w_up, l, e0)
            gates_b = jnp.broadcast_to(gates[e0:e0 + eg, :, None], idx_g.shape + (LANES,))
            ye = _expert_down(hid, w_down, gates_b, l, e0)
            x = x.at[idx_g.reshape(-1)].add(ye.reshape(-1, d))

    y = _rmsnorm(x, final_g)
    return y[:n_p].reshape(bp, sp, d), y[n_p:].reshape(bs, ss, d)
```

```python
import functools
import math

import jax
import jax.numpy as jnp
from jax import lax
from jax.experimental import pallas as pl
from jax.experimental.pallas import tpu as pltpu

EPS = 1e-6
CONV_KERNEL = 31
CONV_PAD = CONV_KERNEL // 2
HALO = 16
HEAD_DIM = 64
V_DIM = 2 * HEAD_DIM
N_BUCKETS = 32
MAX_DISTANCE = 128
CAPACITY_FACTOR = 2
LANES = 128
SUBLANES = 8
EXPERT_GROUP = 16
KV_GROUP = 16
ONES_ROWS = 16
LOG2E = math.log2(math.e)
VMEM_LIMIT_BYTES = 56 * 1024 * 1024

_NT = (((1,), (1,)), ((), ()))


def _tile(n, pref):
    t = min(n, pref)
    while n % t:
        t -= 1
    return t


def _params(*sem):
    return pltpu.CompilerParams(dimension_semantics=sem, vmem_limit_bytes=VMEM_LIMIT_BYTES)


def _norm_matmul_kernel(x_ref, g_ref, w_ref, s_ref, o_ref, h_ref):
    @pl.when(pl.program_id(1) == 0)
    def _():
        x = x_ref[...]
        ms = jnp.mean(x * x, axis=-1, keepdims=True)
        h_ref[...] = (x * lax.rsqrt(ms + EPS) * g_ref[...]).astype(h_ref.dtype)

    y = jnp.dot(h_ref[...], w_ref[...], preferred_element_type=jnp.float32)
    o_ref[...] = (y * s_ref[...]).astype(o_ref.dtype)


def _norm_matmul(x, g, w, col_scale, out_dtype):
    t, d = x.shape
    n = w.shape[1]
    tm, tn = _tile(t, 1024), _tile(n, 1024)
    return pl.pallas_call(
        _norm_matmul_kernel,
        grid=(t // tm, n // tn),
        in_specs=[
            pl.BlockSpec((tm, d), lambda i, j: (i, 0)),
            pl.BlockSpec((1, d), lambda i, j: (0, 0)),
            pl.BlockSpec((d, tn), lambda i, j: (0, j)),
            pl.BlockSpec((1, tn), lambda i, j: (0, j)),
        ],
        out_specs=pl.BlockSpec((tm, tn), lambda i, j: (i, j)),
        out_shape=jax.ShapeDtypeStruct((t, n), out_dtype),
        scratch_shapes=[pltpu.VMEM((tm, d), jnp.bfloat16)],
        compiler_params=_params("parallel", "arbitrary"),
        name="norm_matmul",
    )(x, g.reshape(1, d), w, col_scale.reshape(1, n))


def _conv_kernel(prev_ref, cur_ref, next_ref, w_ref, b_ref, lg_ref, lb_ref, o_ref, e_ref, es_ref, c_ref,
                 *, ts, cw, n_p, s_p, s_s):
    r0 = pl.program_id(0) * ts
    in_p = r0 < n_p
    local = jnp.where(in_p, r0 % s_p, (r0 - n_p) % s_s)
    slen = jnp.where(in_p, s_p, s_s)
    keep_prev = (local != 0).astype(jnp.float32)
    keep_next = (local + ts != slen).astype(jnp.float32)

    def glu(z):
        return z[:, :cw] * jax.nn.sigmoid(z[:, cw:])

    e_ref[0:HALO, :] = glu(prev_ref[...]) * keep_prev
    e_ref[HALO:HALO + ts, :] = glu(cur_ref[...])
    e_ref[HALO + ts:, :] = glu(next_ref[...]) * keep_next

    off = HALO - CONV_PAD
    rows = es_ref.shape[1]
    for r in range(1, SUBLANES):
        es_ref[r - 1] = e_ref[r:r + rows, :]
    for c in range(cw // LANES):
        cs = slice(c * LANES, (c + 1) * LANES)
        acc = jnp.broadcast_to(b_ref[:, cs], (ts, LANES))
        for r in range(SUBLANES):
            x = e_ref[0:rows, cs] if r == 0 else es_ref[r - 1, :, cs]
            for base in range(0, rows - ts + 1, SUBLANES):
                j = base + r - off
                if 0 <= j < CONV_KERNEL:
                    acc = acc + w_ref[j:j + 1, cs] * x[base:base + ts]
        c_ref[:, cs] = acc

    u = c_ref[...]
    mu = jnp.mean(u, axis=-1, keepdims=True)
    var = jnp.mean(jnp.square(u - mu), axis=-1, keepdims=True)
    y = (u - mu) * lax.rsqrt(var + EPS) * lg_ref[...] + lb_ref[...]
    o_ref[...] = (y * jax.nn.sigmoid(y)).astype(o_ref.dtype)


def _conv_module(zc, conv_w, conv_b, ln_g, ln_b, n_p, s_p, s_s):
    t = zc.shape[0]
    cw = zc.shape[1] // 2
    ts = _tile(math.gcd(s_p, s_s), 256)
    hb = ts // HALO
    last_hb = t // HALO - 1
    shifted_extra = (HALO - CONV_PAD + CONV_KERNEL - 1) // SUBLANES * SUBLANES
    kern = functools.partial(_conv_kernel, ts=ts, cw=cw, n_p=n_p, s_p=s_p, s_s=s_s)
    return pl.pallas_call(
        kern,
        grid=(t // ts,),
        in_specs=[
            pl.BlockSpec((HALO, 2 * cw), lambda i: (jnp.maximum(i * hb - 1, 0), 0)),
            pl.BlockSpec((ts, 2 * cw), lambda i: (i, 0)),
            pl.BlockSpec((HALO, 2 * cw), lambda i: (jnp.minimum((i + 1) * hb, last_hb), 0)),
            pl.BlockSpec((CONV_KERNEL, cw), lambda i: (0, 0)),
            pl.BlockSpec((1, cw), lambda i: (0, 0)),
            pl.BlockSpec((1, cw), lambda i: (0, 0)),
            pl.BlockSpec((1, cw), lambda i: (0, 0)),
        ],
        out_specs=pl.BlockSpec((ts, cw), lambda i: (i, 0)),
        out_shape=jax.ShapeDtypeStruct((t, cw), jnp.bfloat16),
        scratch_shapes=[pltpu.VMEM((ts + 2 * HALO, cw), jnp.float32),
                        pltpu.VMEM((SUBLANES - 1, ts + shifted_extra, cw), jnp.float32),
                        pltpu.VMEM((ts, cw), jnp.float32)],
        compiler_params=_params("parallel"),
        name="conv_module",
    )(zc, zc, zc, conv_w, conv_b.reshape(1, cw), ln_g.reshape(1, cw), ln_b.reshape(1, cw))


def _rel_bucket(rel):
    half = N_BUCKETS // 2
    max_exact = half // 2
    ret = (rel > 0).astype(jnp.int32) * half
    n = jnp.abs(rel)
    nf = jnp.maximum(n, 1).astype(jnp.float32)
    large = max_exact + (jnp.log(nf / max_exact) / math.log(MAX_DISTANCE / max_exact)
                         * (half - max_exact)).astype(jnp.int32)
    large = jnp.minimum(large, half - 1)
    return ret + jnp.where(n < max_exact, n, large)


def _bias_tiles(rel_bias, tq, tk):
    assert tk > MAX_DISTANCE and tq % tk == 0
    d = jnp.arange(-2, tq // tk + 2, dtype=jnp.int32)[:, None, None]
    kk = jnp.arange(tk, dtype=jnp.int32)[None, :, None]
    qq = jnp.arange(tq, dtype=jnp.int32)[None, None, :]
    buckets = _rel_bucket(d * tk + kk - qq)[None]
    table = rel_bias.astype(jnp.float32).T * LOG2E
    out = jnp.zeros((table.shape[0],) + buckets.shape[1:], jnp.float32)
    for bkt in range(N_BUCKETS):
        out = jnp.where(buckets == bkt, table[:, bkt][:, None, None, None], out)
    return out


def _attn_kernel(lam_ref, q_ref, k_ref, vt_ref, bias_ref, g_ref, o_ref, qm_ref, st_ref, sc_ref,
                 *, tq, tk, post_scale):
    i = pl.program_id(1)
    nkv = k_ref.shape[0] // tk
    strips = tq // tk
    q = q_ref[...]
    lane = lax.broadcasted_iota(jnp.int32, q.shape, 1)
    zero = jnp.zeros_like(q)
    qm_ref[0] = jnp.where(lane < HEAD_DIM, q, zero)
    qm_ref[1] = jnp.where(lane >= HEAD_DIM, q, zero)
    r_m, r_l = V_DIM, V_DIM + SUBLANES
    st_ref[:, :r_m, :] = jnp.zeros((2, r_m, tq), jnp.float32)
    st_ref[:, r_m:r_l, :] = jnp.full((2, SUBLANES, tq), -jnp.inf, jnp.float32)
    st_ref[:, r_l:, :] = jnp.zeros((2, SUBLANES, tq), jnp.float32)

    chains = [(mp, slice(s_i * tk, (s_i + 1) * tk)) for mp in range(2) for s_i in range(strips)]

    def scores(c):
        k = k_ref[pl.ds(pl.multiple_of(c * tk, tk), tk), :]
        return jnp.stack([lax.dot_general(k, qm_ref[mp, qs, :], _NT, preferred_element_type=jnp.float32)
                          for mp, qs in chains])

    sc_ref[0] = scores(0)

    def body(c, _):
        nxt = scores(jnp.minimum(c + 1, nkv - 1))
        cur = sc_ref[c % 2]
        vt = vt_ref[0, c]
        tile = jnp.clip(c - i * strips, -2, strips + 1) + 2
        st = st_ref[...]
        stats, probs = [], []
        for n, (mp, qs) in enumerate(chains):
            s = cur[n] + bias_ref[0, tile, :, qs]
            m_prev = st[mp, r_m:r_l, qs]
            m_new = jnp.maximum(m_prev, jnp.max(s, axis=0, keepdims=True))
            stats.append((jnp.exp2(m_prev - m_new), m_new))
            probs.append(jnp.exp2(s - m_new[:1]).astype(vt.dtype))
        blocks = []
        for (mp, qs), p, (alpha, m_new) in zip(chains, probs, stats):
            pv = jnp.dot(vt, p, preferred_element_type=jnp.float32)
            a_new = alpha[:1] * st[mp, :r_m, qs] + pv[:r_m]
            l_new = alpha * st[mp, r_l:, qs] + pv[r_m:r_l]
            blocks.append(jnp.concatenate([a_new, m_new, l_new], axis=0))
        st_ref[...] = jnp.stack([jnp.concatenate(blocks[mp * strips:(mp + 1) * strips], axis=1)
                                 for mp in range(2)])
        sc_ref[(c + 1) % 2] = nxt
        return 0

    lax.fori_loop(0, nkv, body, 0, unroll=_tile(nkv, min(KV_GROUP, max(nkv // 2, 1))))

    l1, l2 = st_ref[0, r_l:r_l + 1, :], st_ref[1, r_l:r_l + 1, :]
    o = st_ref[0, :r_m, :] / l1 - lam_ref[0] * (st_ref[1, :r_m, :] / l2)
    ms = jnp.mean(o * o, axis=0, keepdims=True)
    o = (o * lax.rsqrt(ms + EPS)).T * (g_ref[...] * post_scale)
    o_ref[...] = o.astype(o_ref.dtype)


def _diff_attention(zqkv, vt, row0, batch, seq, n_heads, lam, bias_tiles, subln_g, post_scale):
    _, n_tiles, tk, tq = bias_tiles.shape
    nt = seq // tq
    rb0 = row0 // tq
    kern = functools.partial(_attn_kernel, tq=tq, tk=tk, post_scale=post_scale)
    return pl.pallas_call(
        kern,
        grid=(batch * n_heads, nt),
        in_specs=[
            pl.BlockSpec(memory_space=pltpu.SMEM),
            pl.BlockSpec((tq, V_DIM), lambda bh, i: (rb0 + (bh // n_heads) * nt + i, bh % n_heads)),
            pl.BlockSpec((seq, V_DIM), lambda bh, i: (row0 // seq + bh // n_heads, n_heads + bh % n_heads)),
            pl.BlockSpec((1, seq // tk, vt.shape[2], tk),
                         lambda bh, i: (bh % n_heads, row0 // seq + bh // n_heads, 0, 0)),
            pl.BlockSpec((1, n_tiles, tk, tq), lambda bh, i: (bh % n_heads, 0, 0, 0)),
            pl.BlockSpec((1, V_DIM), lambda bh, i: (0, 0)),
        ],
        out_specs=pl.BlockSpec((tq, V_DIM), lambda bh, i: ((bh // n_heads) * nt + i, bh % n_heads)),
        out_shape=jax.ShapeDtypeStruct((batch * seq, n_heads * V_DIM), jnp.bfloat16),
        scratch_shapes=[pltpu.VMEM((2, tq, V_DIM), jnp.bfloat16),
                        pltpu.VMEM((2, V_DIM + 2 * SUBLANES, tq), jnp.float32),
                        pltpu.VMEM((2, 2 * tq // tk, tk, tk), jnp.float32)],
        compiler_params=_params("parallel", "arbitrary"),
        name="diff_attention",
    )(lam.reshape(1), zqkv, zqkv, vt, bias_tiles, subln_g.reshape(1, V_DIM))


def _outproj_kernel(x_ref, u_ref, a_ref, wu_ref, wa_ref, g_ref, wr_ref, x1_ref, h_ref, aff_ref, *, n_experts):
    y = jnp.dot(u_ref[...], wu_ref[...], preferred_element_type=jnp.float32)
    y = y + jnp.dot(a_ref[...], wa_ref[...], preferred_element_type=jnp.float32)
    x1 = x_ref[...] + y
    x1_ref[...] = x1
    ms = jnp.mean(x1 * x1, axis=-1, keepdims=True)
    h = (x1 * lax.rsqrt(ms + EPS) * g_ref[...]).astype(h_ref.dtype)
    h_ref[...] = h
    logits = jnp.dot(h, wr_ref[...], preferred_element_type=jnp.float32)
    lane = lax.broadcasted_iota(jnp.int32, logits.shape, 1)
    logits = jnp.where(lane < n_experts, logits, -jnp.inf)
    e = jnp.exp(logits - jnp.max(logits, axis=-1, keepdims=True))
    aff_ref[...] = e / jnp.sum(e, axis=-1, keepdims=True)


def _outproj_router(x, u, a, w_u, w_a, g, w_r, n_experts):
    t, d = x.shape
    cw, aw = u.shape[1], a.shape[1]
    tm = _tile(t, 256)
    kern = functools.partial(_outproj_kernel, n_experts=n_experts)
    return pl.pallas_call(
        kern,
        grid=(t // tm,),
        in_specs=[
            pl.BlockSpec((tm, d), lambda i: (i, 0)),
            pl.BlockSpec((tm, cw), lambda i: (i, 0)),
            pl.BlockSpec((tm, aw), lambda i: (i, 0)),
            pl.BlockSpec((cw, d), lambda i: (0, 0)),
            pl.BlockSpec((aw, d), lambda i: (0, 0)),
            pl.BlockSpec((1, d), lambda i: (0, 0)),
            pl.BlockSpec((d, LANES), lambda i: (0, 0)),
        ],
        out_specs=[
            pl.BlockSpec((tm, d), lambda i: (i, 0)),
            pl.BlockSpec((tm, d), lambda i: (i, 0)),
            pl.BlockSpec((tm, LANES), lambda i: (i, 0)),
        ],
        out_shape=[
            jax.ShapeDtypeStruct((t, d), jnp.float32),
            jax.ShapeDtypeStruct((t, d), jnp.bfloat16),
            jax.ShapeDtypeStruct((t, LANES), jnp.float32),
        ],
        compiler_params=_params("parallel"),
        name="outproj_router",
    )(x, u, a, w_u, w_a, g.reshape(1, d), w_r)


def _gate_up_kernel(x_ref, wg_ref, wu_ref, o_ref):
    x = x_ref[0]
    g = jnp.dot(x, wg_ref[0, 0].astype(x.dtype), preferred_element_type=jnp.float32)
    u = jnp.dot(x, wu_ref[0, 0].astype(x.dtype), preferred_element_type=jnp.float32)
    o_ref[0] = (g * jax.nn.sigmoid(g) * u).astype(o_ref.dtype)


def _expert_gate_up(xe, w_gate, w_up, layer, e0):
    e, m, d = xe.shape
    f = w_gate.shape[3]
    tm, tn = _tile(m, 1536), _tile(f, 512)
    return pl.pallas_call(
        _gate_up_kernel,
        grid=(e, f // tn, m // tm),
        in_specs=[
            pl.BlockSpec((1, tm, d), lambda ei, n, mi: (ei, mi, 0)),
            pl.BlockSpec((1, 1, d, tn), lambda ei, n, mi: (layer, e0 + ei, 0, n)),
            pl.BlockSpec((1, 1, d, tn), lambda ei, n, mi: (layer, e0 + ei, 0, n)),
        ],
        out_specs=pl.BlockSpec((1, tm, tn), lambda ei, n, mi: (ei, mi, n)),
        out_shape=jax.ShapeDtypeStruct((e, m, f), jnp.bfloat16),
        compiler_params=_params("parallel", "parallel", "arbitrary"),
        name="expert_gate_up",
    )(xe, w_gate, w_up)


def _down_kernel(h_ref, w_ref, gate_ref, o_ref):
    h = h_ref[0]
    y = jnp.dot(h, w_ref[0, 0].astype(h.dtype), preferred_element_type=jnp.float32)
    reps = y.shape[1] // LANES
    o_ref[0] = y * jnp.concatenate([gate_ref[0]] * reps, axis=1)


def _expert_down(h, w_down, gates_b, layer, e0):
    e, m, f = h.shape
    d = w_down.shape[3]
    tm, tn = _tile(m, 1024), _tile(d, 512)
    return pl.pallas_call(
        _down_kernel,
        grid=(e, m // tm, d // tn),
        in_specs=[
            pl.BlockSpec((1, tm, f), lambda ei, mi, n: (ei, mi, 0)),
            pl.BlockSpec((1, 1, f, tn), lambda ei, mi, n: (layer, e0 + ei, 0, n)),
            pl.BlockSpec((1, tm, LANES), lambda ei, mi, n: (ei, mi, 0)),
        ],
        out_specs=pl.BlockSpec((1, tm, tn), lambda ei, mi, n: (ei, mi, n)),
        out_shape=jax.ShapeDtypeStruct((e, m, d), jnp.float32),
        compiler_params=_params("parallel", "parallel", "arbitrary"),
        name="expert_down",
    )(h, w_down, gates_b)


def _rmsnorm_kernel(x_ref, g_ref, o_ref):
    x = x_ref[...]
    ms = jnp.mean(x * x, axis=-1, keepdims=True)
    o_ref[...] = x * lax.rsqrt(ms + EPS) * g_ref[...]


def _rmsnorm(x, g):
    t, d = x.shape
    tm = _tile(t, 512)
    return pl.pallas_call(
        _rmsnorm_kernel,
        grid=(t // tm,),
        in_specs=[pl.BlockSpec((tm, d), lambda i: (i, 0)), pl.BlockSpec((1, d), lambda i: (0, 0))],
        out_specs=pl.BlockSpec((tm, d), lambda i: (i, 0)),
        out_shape=jax.ShapeDtypeStruct((t, d), jnp.float32),
        compiler_params=_params("parallel"),
        name="final_rmsnorm",
    )(x, g.reshape(1, d))


def _route(aff, n_experts):
    n = aff.shape[0]
    cap = CAPACITY_FACTOR * n // n_experts
    gates, idx = lax.top_k(aff[:, :n_experts].T, cap)
    idx, gates = lax.sort((idx, gates), dimension=1, num_keys=1)
    return gates, idx


def kernel(x_prompt, x_sample, rel_bias, final_g, ln1_g, w_in, conv_w, conv_b, conv_ln_g, conv_ln_b,
           lam_q1, lam_k1, lam_q2, lam_k2, subln_g, w_out, ln2_g, w_router, w_gate, w_up, w_down):
    bp, sp, d = x_prompt.shape
    bs, ss, _ = x_sample.shape
    n_p, n_s = bp * sp, bs * ss
    depth = w_in.shape[0]
    cw = conv_w.shape[2]
    n_heads = (w_in.shape[2] - 2 * cw) // (3 * V_DIM)
    n_experts = w_router.shape[2]
    bf16 = jnp.bfloat16

    x = jnp.concatenate([x_prompt.reshape(n_p, d), x_sample.reshape(n_s, d)], axis=0)
    tq_attn = _tile(math.gcd(sp, ss), 512)
    tk_attn = _tile(tq_attn, 256)
    bias_tiles = _bias_tiles(rel_bias, tq_attn, tk_attn)
    qk_w = n_heads * V_DIM
    qkv_scale = jnp.concatenate([jnp.full((qk_w,), LOG2E * HEAD_DIM ** -0.5, jnp.float32),
                                 jnp.ones((2 * qk_w,), jnp.float32)])

    for l in range(depth):
        w_in_l = w_in[l].astype(bf16)
        zc = _norm_matmul(x, ln1_g[l], w_in_l[:, :2 * cw], jnp.ones((2 * cw,), jnp.float32), jnp.float32)
        zqkv = _norm_matmul(x, ln1_g[l], w_in_l[:, 2 * cw:], qkv_scale, bf16)

        u = _conv_module(zc, conv_w[l], conv_b[l], conv_ln_g[l], conv_ln_b[l], n_p, sp, ss)

        lam_init = 0.8 - 0.6 * math.exp(-0.3 * l)
        lam = (jnp.exp(jnp.sum(lam_q1[l] * lam_k1[l])) - jnp.exp(jnp.sum(lam_q2[l] * lam_k2[l])) + lam_init)
        vt = zqkv[:, 2 * qk_w:].reshape((n_p + n_s) // tk_attn, tk_attn, n_heads, V_DIM).transpose(2, 0, 3, 1)
        vt = jnp.concatenate([vt, jnp.ones(vt.shape[:2] + (ONES_ROWS, tk_attn), bf16)], axis=2)
        attn = functools.partial(_diff_attention, zqkv, vt, n_heads=n_heads, lam=lam, bias_tiles=bias_tiles,
                                 subln_g=subln_g[l], post_scale=1.0 - lam_init)
        a = jnp.concatenate([attn(row0=0, batch=bp, seq=sp), attn(row0=n_p, batch=bs, seq=ss)], axis=0)

        w_out_l = w_out[l].astype(bf16)
        w_r = jnp.pad(w_router[l], ((0, 0), (0, LANES - n_experts))).astype(bf16)
        x1, h, aff = _outproj_router(x, u, a, w_out_l[:cw], w_out_l[cw:], ln2_g[l], w_r, n_experts)

        gates_p, idx_p = _route(aff[:n_p], n_experts)
        gates_s, idx_s = _route(aff[n_p:], n_experts)
        gates = jnp.concatenate([gates_p, gates_s], axis=1)
        idx = jnp.concatenate([idx_p, idx_s + n_p], axis=1)

        x = x1
        eg = _tile(n_experts, EXPERT_GROUP)
        for e0 in range(0, n_experts, eg):
            idx_g = idx[e0:e0 + eg]
            hid = _expert_gate_up(h[idx_g], w_gate, w_up, l, e0)
            gates_b = jnp.broadcast_to(gates[e0:e0 + eg, :, None], idx_g.shape + (LANES,))
            ye = _expert_down(hid, w_down, gates_b, l, e0)
            x = x.at[idx_g.reshape(-1)].add(ye.reshape(-1, d))

    y = _rmsnorm(x, final_g)
    return y[:n_p].reshape(bp, sp, d), y[n_p:].reshape(bs, ss, d)
```

```python
import functools
import math

import jax
import jax.numpy as jnp
from jax import lax
from jax.experimental import pallas as pl
from jax.experimental.pallas import tpu as pltpu

EPS = 1e-6
CONV_KERNEL = 31
CONV_PAD = CONV_KERNEL // 2
HALO = 16
HEAD_DIM = 64
V_DIM = 2 * HEAD_DIM
N_BUCKETS = 32
MAX_DISTANCE = 128
CAPACITY_FACTOR = 2
LANES = 128
SUBLANES = 8
EXPERT_GROUP = 16
KV_GROUP = 16
ONES_ROWS = 16
LOG2E = math.log2(math.e)
VMEM_LIMIT_BYTES = 56 * 1024 * 1024

_NT = (((1,), (1,)), ((), ()))


def _tile(n, pref):
    t = min(n, pref)
    while n % t:
        t -= 1
    return t


def _params(*sem):
    return pltpu.CompilerParams(dimension_semantics=sem, vmem_limit_bytes=VMEM_LIMIT_BYTES)


def _norm_matmul_kernel(x_ref, g_ref, w_ref, s_ref, o_ref, h_ref):
    @pl.when(pl.program_id(1) == 0)
    def _():
        x = x_ref[...]
        ms = jnp.mean(x * x, axis=-1, keepdims=True)
        h_ref[...] = (x * lax.rsqrt(ms + EPS) * g_ref[...]).astype(h_ref.dtype)

    y = jnp.dot(h_ref[...], w_ref[...], preferred_element_type=jnp.float32)
    o_ref[...] = (y * s_ref[...]).astype(o_ref.dtype)


def _norm_matmul(x, g, w, col_scale, out_dtype):
    t, d = x.shape
    n = w.shape[1]
    tm, tn = _tile(t, 1024), _tile(n, 1024)
    return pl.pallas_call(
        _norm_matmul_kernel,
        grid=(t // tm, n // tn),
        in_specs=[
            pl.BlockSpec((tm, d), lambda i, j: (i, 0)),
            pl.BlockSpec((1, d), lambda i, j: (0, 0)),
            pl.BlockSpec((d, tn), lambda i, j: (0, j)),
            pl.BlockSpec((1, tn), lambda i, j: (0, j)),
        ],
        out_specs=pl.BlockSpec((tm, tn), lambda i, j: (i, j)),
        out_shape=jax.ShapeDtypeStruct((t, n), out_dtype),
        scratch_shapes=[pltpu.VMEM((tm, d), jnp.bfloat16)],
        compiler_params=_params("parallel", "arbitrary"),
        name="norm_matmul",
    )(x, g.reshape(1, d), w, col_scale.reshape(1, n))


def _conv_kernel(prev_ref, cur_ref, next_ref, w_ref, b_ref, lg_ref, lb_ref, o_ref, e_ref, es_ref, c_ref,
                 *, ts, cw, n_p, s_p, s_s):
    r0 = pl.program_id(0) * ts
    in_p = r0 < n_p
    local = jnp.where(in_p, r0 % s_p, (r0 - n_p) % s_s)
    slen = jnp.where(in_p, s_p, s_s)
    keep_prev = (local != 0).astype(jnp.float32)
    keep_next = (local + ts != slen).astype(jnp.float32)

    def glu(z):
        return z[:, :cw] * jax.nn.sigmoid(z[:, cw:])

    e_ref[0:HALO, :] = glu(prev_ref[...]) * keep_prev
    e_ref[HALO:HALO + ts, :] = glu(cur_ref[...])
    e_ref[HALO + ts:, :] = glu(next_ref[...]) * keep_next

    off = HALO - CONV_PAD
    rows = es_ref.shape[1]
    for r in range(1, SUBLANES):
        es_ref[r - 1] = e_ref[r:r + rows, :]
    for c in range(cw // LANES):
        cs = slice(c * LANES, (c + 1) * LANES)
        acc = jnp.broadcast_to(b_ref[:, cs], (ts, LANES))
        for r in range(SUBLANES):
            x = e_ref[0:rows, cs] if r == 0 else es_ref[r - 1, :, cs]
            for base in range(0, rows - ts + 1, SUBLANES):
                j = base + r - off
                if 0 <= j < CONV_KERNEL:
                    acc = acc + w_ref[j:j + 1, cs] * x[base:base + ts]
        c_ref[:, cs] = acc

    u = c_ref[...]
    mu = jnp.mean(u, axis=-1, keepdims=True)
    var = jnp.mean(jnp.square(u - mu), axis=-1, keepdims=True)
    y = (u - mu) * lax.rsqrt(var + EPS) * lg_ref[...] + lb_ref[...]
    o_ref[...] = (y * jax.nn.sigmoid(y)).astype(o_ref.dtype)


def _conv_module(zc, conv_w, conv_b, ln_g, ln_b, n_p, s_p, s_s):
    t = zc.shape[0]
    cw = zc.shape[1] // 2
    ts = _tile(math.gcd(s_p, s_s), 256)
    hb = ts // HALO
    last_hb = t // HALO - 1
    shifted_extra = (HALO - CONV_PAD + CONV_KERNEL - 1) // SUBLANES * SUBLANES
    kern = functools.partial(_conv_kernel, ts=ts, cw=cw, n_p=n_p, s_p=s_p, s_s=s_s)
    return pl.pallas_call(
        kern,
        grid=(t // ts,),
        in_specs=[
            pl.BlockSpec((HALO, 2 * cw), lambda i: (jnp.maximum(i * hb - 1, 0), 0)),
            pl.BlockSpec((ts, 2 * cw), lambda i: (i, 0)),
            pl.BlockSpec((HALO, 2 * cw), lambda i: (jnp.minimum((i + 1) * hb, last_hb), 0)),
            pl.BlockSpec((CONV_KERNEL, cw), lambda i: (0, 0)),
            pl.BlockSpec((1, cw), lambda i: (0, 0)),
            pl.BlockSpec((1, cw), lambda i: (0, 0)),
            pl.BlockSpec((1, cw), lambda i: (0, 0)),
        ],
        out_specs=pl.BlockSpec((ts, cw), lambda i: (i, 0)),
        out_shape=jax.ShapeDtypeStruct((t, cw), jnp.bfloat16),
        scratch_shapes=[pltpu.VMEM((ts + 2 * HALO, cw), jnp.float32),
                        pltpu.VMEM((SUBLANES - 1, ts + shifted_extra, cw), jnp.float32),
                        pltpu.VMEM((ts, cw), jnp.float32)],
        compiler_params=_params("parallel"),
        name="conv_module",
    )(zc, zc, zc, conv_w, conv_b.reshape(1, cw), ln_g.reshape(1, cw), ln_b.reshape(1, cw))


def _rel_bucket(rel):
    half = N_BUCKETS // 2
    max_exact = half // 2
    ret = (rel > 0).astype(jnp.int32) * half
    n = jnp.abs(rel)
    nf = jnp.maximum(n, 1).astype(jnp.float32)
    large = max_exact + (jnp.log(nf / max_exact) / math.log(MAX_DISTANCE / max_exact)
                         * (half - max_exact)).astype(jnp.int32)
    large = jnp.minimum(large, half - 1)
    return ret + jnp.where(n < max_exact, n, large)


def _bias_tiles(rel_bias, tq, tk):
    assert tk > MAX_DISTANCE and tq % tk == 0
    d = jnp.arange(-2, tq // tk + 2, dtype=jnp.int32)[:, None, None]
    kk = jnp.arange(tk, dtype=jnp.int32)[None, :, None]
    qq = jnp.arange(tq, dtype=jnp.int32)[None, None, :]
    buckets = _rel_bucket(d * tk + kk - qq)[None]
    table = rel_bias.astype(jnp.float32).T * LOG2E
    out = jnp.zeros((table.shape[0],) + buckets.shape[1:], jnp.float32)
    for bkt in range(N_BUCKETS):
        out = jnp.where(buckets == bkt, table[:, bkt][:, None, None, None], out)
    return out


def _attn_kernel(lam_ref, q_ref, k_ref, vt_ref, bias_ref, g_ref, o_ref, qm_ref, st_ref, sc_ref,
                 *, tq, tk, post_scale):
    i = pl.program_id(1)
    nkv = k_ref.shape[0] // tk
    strips = tq // tk
    q = q_ref[...]
    lane = lax.broadcasted_iota(jnp.int32, q.shape, 1)
    zero = jnp.zeros_like(q)
    qm_ref[0] = jnp.where(lane < HEAD_DIM, q, zero)
    qm_ref[1] = jnp.where(lane >= HEAD_DIM, q, zero)
    r_m, r_l = V_DIM, V_DIM + SUBLANES
    st_ref[:, :r_m, :] = jnp.zeros((2, r_m, tq), jnp.float32)
    st_ref[:, r_m:r_l, :] = jnp.full((2, SUBLANES, tq), -jnp.inf, jnp.float32)
    st_ref[:, r_l:, :] = jnp.zeros((2, SUBLANES, tq), jnp.float32)

    chains = [(mp, slice(s_i * tk, (s_i + 1) * tk)) for mp in range(2) for s_i in range(strips)]

    def scores(c):
        k = k_ref[pl.ds(pl.multiple_of(c * tk, tk), tk), :]
        return jnp.stack([lax.dot_general(k, qm_ref[mp, qs, :], _NT, preferred_element_type=jnp.float32)
                          for mp, qs in chains])

    sc_ref[0] = scores(0)

    def body(c, _):
        nxt = scores(jnp.minimum(c + 1, nkv - 1))
        cur = sc_ref[c % 2]
        vt = vt_ref[0, c]
        tile = jnp.clip(c - i * strips, -2, strips + 1) + 2
        st = st_ref[...]
        stats, probs = [], []
        for n, (mp, qs) in enumerate(chains):
            s = cur[n] + bias_ref[0, tile, :, qs]
            m_prev = st[mp, r_m:r_l, qs]
            m_new = jnp.maximum(m_prev, jnp.max(s, axis=0, keepdims=True))
            stats.append((jnp.exp2(m_prev - m_new), m_new))
            probs.append(jnp.exp2(s - m_new[:1]).astype(vt.dtype))
        blocks = []
        for (mp, qs), p, (alpha, m_new) in zip(chains, probs, stats):
            pv = jnp.dot(vt, p, preferred_element_type=jnp.float32)
            a_new = alpha[:1] * st[mp, :r_m, qs] + pv[:r_m]
            l_new = alpha * st[mp, r_l:, qs] + pv[r_m:r_l]
            blocks.append(jnp.concatenate([a_new, m_new, l_new], axis=0))
        st_ref[...] = jnp.stack([jnp.concatenate(blocks[mp * strips:(mp + 1) * strips], axis=1)
                                 for mp in range(2)])
        sc_ref[(c + 1) % 2] = nxt
        return 0

    lax.fori_loop(0, nkv, body, 0, unroll=_tile(nkv, min(KV_GROUP, max(nkv // 2, 1))))

    inv1 = 1.0 / st_ref[0, r_l:r_l + 1, :]
    inv2 = lam_ref[0] / st_ref[1, r_l:r_l + 1, :]
    o = st_ref[0, :r_m, :] * inv1 - st_ref[1, :r_m, :] * inv2
    ms = jnp.mean(o * o, axis=0, keepdims=True)
    o = (o * lax.rsqrt(ms + EPS)).T * (g_ref[...] * post_scale)
    o_ref[...] = o.astype(o_ref.dtype)


def _diff_attention(zqkv, vt, row0, batch, seq, n_heads, lam, bias_tiles, subln_g, post_scale):
    _, n_tiles, tk, tq = bias_tiles.shape
    nt = seq // tq
    rb0 = row0 // tq
    kern = functools.partial(_attn_kernel, tq=tq, tk=tk, post_scale=post_scale)
    return pl.pallas_call(
        kern,
        grid=(batch * n_heads, nt),
        in_specs=[
            pl.BlockSpec(memory_space=pltpu.SMEM),
            pl.BlockSpec((tq, V_DIM), lambda bh, i: (rb0 + (bh // n_heads) * nt + i, bh % n_heads)),
            pl.BlockSpec((seq, V_DIM), lambda bh, i: (row0 // seq + bh // n_heads, n_heads + bh % n_heads)),
            pl.BlockSpec((1, seq // tk, vt.shape[2], tk),
                         lambda bh, i: (bh % n_heads, row0 // seq + bh // n_heads, 0, 0)),
            pl.BlockSpec((1, n_tiles, tk, tq), lambda bh, i: (bh % n_heads, 0, 0, 0)),
            pl.BlockSpec((1, V_DIM), lambda bh, i: (0, 0)),
        ],
        out_specs=pl.BlockSpec((tq, V_DIM), lambda bh, i: ((bh // n_heads) * nt + i, bh % n_heads)),
        out_shape=jax.ShapeDtypeStruct((batch * seq, n_heads * V_DIM), jnp.bfloat16),
        scratch_shapes=[pltpu.VMEM((2, tq, V_DIM), jnp.bfloat16),
                        pltpu.VMEM((2, V_DIM + 2 * SUBLANES, tq), jnp.float32),
                        pltpu.VMEM((2, 2 * tq // tk, tk, tk), jnp.float32)],
        compiler_params=_params("parallel", "arbitrary"),
        name="diff_attention",
    )(lam.reshape(1), zqkv, zqkv, vt, bias_tiles, subln_g.reshape(1, V_DIM))


def _outproj_kernel(x_ref, u_ref, a_ref, wu_ref, wa_ref, g_ref, wr_ref, x1_ref, h_ref, aff_ref, *, n_experts):
    y = jnp.dot(u_ref[...], wu_ref[...], preferred_element_type=jnp.float32)
    y = y + jnp.dot(a_ref[...], wa_ref[...], preferred_element_type=jnp.float32)
    x1 = x_ref[...] + y
    x1_ref[...] = x1
    ms = jnp.mean(x1 * x1, axis=-1, keepdims=True)
    h = (x1 * lax.rsqrt(ms + EPS) * g_ref[...]).astype(h_ref.dtype)
    h_ref[...] = h
    logits = jnp.dot(h, wr_ref[...], preferred_element_type=jnp.float32)
    lane = lax.broadcasted_iota(jnp.int32, logits.shape, 1)
    logits = jnp.where(lane < n_experts, logits, -jnp.inf)
    e = jnp.exp(logits - jnp.max(logits, axis=-1, keepdims=True))
    aff_ref[...] = e / jnp.sum(e, axis=-1, keepdims=True)


def _outproj_router(x, u, a, w_u, w_a, g, w_r, n_experts):
    t, d = x.shape
    cw, aw = u.shape[1], a.shape[1]
    tm = _tile(t, 256)
    kern = functools.partial(_outproj_kernel, n_experts=n_experts)
    return pl.pallas_call(
        kern,
        grid=(t // tm,),
        in_specs=[
            pl.BlockSpec((tm, d), lambda i: (i, 0)),
            pl.BlockSpec((tm, cw), lambda i: (i, 0)),
            pl.BlockSpec((tm, aw), lambda i: (i, 0)),
            pl.BlockSpec((cw, d), lambda i: (0, 0)),
            pl.BlockSpec((aw, d), lambda i: (0, 0)),
            pl.BlockSpec((1, d), lambda i: (0, 0)),
            pl.BlockSpec((d, LANES), lambda i: (0, 0)),
        ],
        out_specs=[
            pl.BlockSpec((tm, d), lambda i: (i, 0)),
            pl.BlockSpec((tm, d), lambda i: (i, 0)),
            pl.BlockSpec((tm, LANES), lambda i: (i, 0)),
        ],
        out_shape=[
            jax.ShapeDtypeStruct((t, d), jnp.float32),
            jax.ShapeDtypeStruct((t, d), jnp.bfloat16),
            jax.ShapeDtypeStruct((t, LANES), jnp.float32),
        ],
        compiler_params=_params("parallel"),
        name="outproj_router",
    )(x, u, a, w_u, w_a, g.reshape(1, d), w_r)


def _gate_up_kernel(x_ref, wg_ref, wu_ref, o_ref):
    x = x_ref[0]
    g = jnp.dot(x, wg_ref[0, 0].astype(x.dtype), preferred_element_type=jnp.float32)
    u = jnp.dot(x, wu_ref[0, 0].astype(x.dtype), preferred_element_type=jnp.float32)
    o_ref[0] = (g * jax.nn.sigmoid(g) * u).astype(o_ref.dtype)


def _expert_gate_up(xe, w_gate, w_up, layer, e0):
    e, m, d = xe.shape
    f = w_gate.shape[3]
    tm, tn = _tile(m, 1536), _tile(f, 512)
    return pl.pallas_call(
        _gate_up_kernel,
        grid=(e, f // tn, m // tm),
        in_specs=[
            pl.BlockSpec((1, tm, d), lambda ei, n, mi: (ei, mi, 0)),
            pl.BlockSpec((1, 1, d, tn), lambda ei, n, mi: (layer, e0 + ei, 0, n)),
            pl.BlockSpec((1, 1, d, tn), lambda ei, n, mi: (layer, e0 + ei, 0, n)),
        ],
        out_specs=pl.BlockSpec((1, tm, tn), lambda ei, n, mi: (ei, mi, n)),
        out_shape=jax.ShapeDtypeStruct((e, m, f), jnp.bfloat16),
        compiler_params=_params("parallel", "parallel", "arbitrary"),
        name="expert_gate_up",
    )(xe, w_gate, w_up)


def _down_kernel(h_ref, w_ref, gate_ref, o_ref):
    h = h_ref[0]
    y = jnp.dot(h, w_ref[0, 0].astype(h.dtype), preferred_element_type=jnp.float32)
    reps = y.shape[1] // LANES
    o_ref[0] = y * jnp.concatenate([gate_ref[0]] * reps, axis=1)


def _expert_down(h, w_down, gates_b, layer, e0):
    e, m, f = h.shape
    d = w_down.shape[3]
    tm, tn = _tile(m, 1024), _tile(d, 512)
    return pl.pallas_call(
        _down_kernel,
        grid=(e, m // tm, d // tn),
        in_specs=[
            pl.BlockSpec((1, tm, f), lambda ei, mi, n: (ei, mi, 0)),
            pl.BlockSpec((1, 1, f, tn), lambda ei, mi, n: (layer, e0 + ei, 0, n)),
            pl.BlockSpec((1, tm, LANES), lambda ei, mi, n: (ei, mi, 0)),
        ],
        out_specs=pl.BlockSpec((1, tm, tn), lambda ei, mi, n: (ei, mi, n)),
        out_shape=jax.ShapeDtypeStruct((e, m, d), jnp.float32),
        compiler_params=_params("parallel", "parallel", "arbitrary"),
        name="expert_down",
    )(h, w_down, gates_b)


def _rmsnorm_kernel(x_ref, g_ref, o_ref):
    x = x_ref[...]
    ms = jnp.mean(x * x, axis=-1, keepdims=True)
    o_ref[...] = x * lax.rsqrt(ms + EPS) * g_ref[...]


def _rmsnorm(x, g, row0, rows):
    d = x.shape[1]
    tm = _tile(math.gcd(row0, rows) if row0 else rows, 512)
    rb0 = row0 // tm
    return pl.pallas_call(
        _rmsnorm_kernel,
        grid=(rows // tm,),
        in_specs=[pl.BlockSpec((tm, d), lambda i: (rb0 + i, 0)), pl.BlockSpec((1, d), lambda i: (0, 0))],
        out_specs=pl.BlockSpec((tm, d), lambda i: (i, 0)),
        out_shape=jax.ShapeDtypeStruct((rows, d), jnp.float32),
        compiler_params=_params("parallel"),
        name="final_rmsnorm",
    )(x, g.reshape(1, d))


def _route(aff, n_experts):
    n = aff.shape[0]
    cap = CAPACITY_FACTOR * n // n_experts
    gates, idx = lax.top_k(aff[:, :n_experts].T, cap)
    idx, gates = lax.sort((idx, gates), dimension=1, num_keys=1)
    return gates, idx


def kernel(x_prompt, x_sample, rel_bias, final_g, ln1_g, w_in, conv_w, conv_b, conv_ln_g, conv_ln_b,
           lam_q1, lam_k1, lam_q2, lam_k2, subln_g, w_out, ln2_g, w_router, w_gate, w_up, w_down):
    bp, sp, d = x_prompt.shape
    bs, ss, _ = x_sample.shape
    n_p, n_s = bp * sp, bs * ss
    depth = w_in.shape[0]
    cw = conv_w.shape[2]
    n_heads = (w_in.shape[2] - 2 * cw) // (3 * V_DIM)
    n_experts = w_router.shape[2]
    bf16 = jnp.bfloat16

    x = jnp.concatenate([x_prompt.reshape(n_p, d), x_sample.reshape(n_s, d)], axis=0)
    tq_attn = _tile(math.gcd(sp, ss), 512)
    tk_attn = _tile(tq_attn, 256)
    bias_tiles = _bias_tiles(rel_bias, tq_attn, tk_attn)
    qk_w = n_heads * V_DIM
    qkv_scale = jnp.concatenate([jnp.full((qk_w,), LOG2E * HEAD_DIM ** -0.5, jnp.float32),
                                 jnp.ones((2 * qk_w,), jnp.float32)])

    for l in range(depth):
        w_in_l = w_in[l].astype(bf16)
        zc = _norm_matmul(x, ln1_g[l], w_in_l[:, :2 * cw], jnp.ones((2 * cw,), jnp.float32), jnp.float32)
        zqkv = _norm_matmul(x, ln1_g[l], w_in_l[:, 2 * cw:], qkv_scale, bf16)

        u = _conv_module(zc, conv_w[l], conv_b[l], conv_ln_g[l], conv_ln_b[l], n_p, sp, ss)

        lam_init = 0.8 - 0.6 * math.exp(-0.3 * l)
        lam = (jnp.exp(jnp.sum(lam_q1[l] * lam_k1[l])) - jnp.exp(jnp.sum(lam_q2[l] * lam_k2[l])) + lam_init)
        vt = zqkv[:, 2 * qk_w:].reshape((n_p + n_s) // tk_attn, tk_attn, n_heads, V_DIM).transpose(2, 0, 3, 1)
        vt = jnp.concatenate([vt, jnp.ones(vt.shape[:2] + (ONES_ROWS, tk_attn), bf16)], axis=2)
        attn = functools.partial(_diff_attention, zqkv, vt, n_heads=n_heads, lam=lam, bias_tiles=bias_tiles,
                                 subln_g=subln_g[l], post_scale=1.0 - lam_init)
        a = jnp.concatenate([attn(row0=0, batch=bp, seq=sp), attn(row0=n_p, batch=bs, seq=ss)], axis=0)

        w_out_l = w_out[l].astype(bf16)
        w_r = jnp.pad(w_router[l], ((0, 0), (0, LANES - n_experts))).astype(bf16)
        x1, h, aff = _outproj_router(x, u, a, w_out_l[:cw], w_out_l[cw:], ln2_g[l], w_r, n_experts)

        gates_p, idx_p = _route(aff[:n_p], n_experts)
        gates_s, idx_s = _route(aff[n_p:], n_experts)
        gates = jnp.concatenate([gates_p, gates_s], axis=1)
        idx = jnp.concatenate([idx_p, idx_s + n_p], axis=1)

        x = x1
        eg = _tile(n_experts, EXPERT_GROUP)
        for e0 in range(0, n_experts, eg):
            idx_g = idx[e0:e0 + eg]
            hid = _expert_gate_up(h[idx_g], w_gate, w_up, l, e0)
            gates_b = jnp.broadcast_to(gates[e0:e0 + eg, :, None], idx_g.shape + (LANES,))
            ye = _expert_down(hid, w_down, gates_b, l, e0)
            x = x.at[idx_g.reshape(-1)].add(ye.reshape(-1, d))

    y_p = _rmsnorm(x, final_g, 0, n_p)
    y_s = _rmsnorm(x, final_g, n_p, n_s)
    return y_p.reshape(bp, sp, d), y_s.reshape(bs, ss, d)
```

```python
import functools
import math

import jax
import jax.numpy as jnp
from jax import lax
from jax.experimental import pallas as pl
from jax.experimental.pallas import tpu as pltpu

EPS = 1e-6
CONV_KERNEL = 31
CONV_PAD = CONV_KERNEL // 2
HALO = 16
HEAD_DIM = 64
V_DIM = 2 * HEAD_DIM
N_BUCKETS = 32
MAX_DISTANCE = 128
CAPACITY_FACTOR = 2
LANES = 128
SUBLANES = 8
EXPERT_GROUP = 16
KV_GROUP = 16
ONES_ROWS = 16
LOG2E = math.log2(math.e)
VMEM_LIMIT_BYTES = 56 * 1024 * 1024

_NT = (((1,), (1,)), ((), ()))


def _tile(n, pref):
    t = min(n, pref)
    while n % t:
        t -= 1
    return t


def _params(*sem):
    return pltpu.CompilerParams(dimension_semantics=sem, vmem_limit_bytes=VMEM_LIMIT_BYTES)


def _norm_matmul_kernel(x_ref, g_ref, w_ref, s_ref, o_ref, h_ref):
    @pl.when(pl.program_id(1) == 0)
    def _():
        x = x_ref[...]
        ms = jnp.mean(x * x, axis=-1, keepdims=True)
        h_ref[...] = (x * lax.rsqrt(ms + EPS) * g_ref[...]).astype(h_ref.dtype)

    y = jnp.dot(h_ref[...], w_ref[...], preferred_element_type=jnp.float32)
    o_ref[...] = (y * s_ref[...]).astype(o_ref.dtype)


def _norm_matmul(x, g, w, col_scale, out_dtype):
    t, d = x.shape
    n = w.shape[1]
    tm, tn = _tile(t, 1024), _tile(n, 1024)
    return pl.pallas_call(
        _norm_matmul_kernel,
        grid=(t // tm, n // tn),
        in_specs=[
            pl.BlockSpec((tm, d), lambda i, j: (i, 0)),
            pl.BlockSpec((1, d), lambda i, j: (0, 0)),
            pl.BlockSpec((d, tn), lambda i, j: (0, j)),
            pl.BlockSpec((1, tn), lambda i, j: (0, j)),
        ],
        out_specs=pl.BlockSpec((tm, tn), lambda i, j: (i, j)),
        out_shape=jax.ShapeDtypeStruct((t, n), out_dtype),
        scratch_shapes=[pltpu.VMEM((tm, d), jnp.bfloat16)],
        compiler_params=_params("parallel", "arbitrary"),
        name="norm_matmul",
    )(x, g.reshape(1, d), w, col_scale.reshape(1, n))


def _conv_kernel(prev_ref, cur_ref, next_ref, w_ref, b_ref, lg_ref, lb_ref, o_ref, e_ref, es_ref, c_ref,
                 *, ts, cw, n_p, s_p, s_s):
    r0 = pl.program_id(0) * ts
    in_p = r0 < n_p
    local = jnp.where(in_p, r0 % s_p, (r0 - n_p) % s_s)
    slen = jnp.where(in_p, s_p, s_s)
    keep_prev = (local != 0).astype(jnp.float32)
    keep_next = (local + ts != slen).astype(jnp.float32)

    def glu(z):
        return z[:, :cw] * jax.nn.sigmoid(z[:, cw:])

    e_ref[0:HALO, :] = glu(prev_ref[...]) * keep_prev
    e_ref[HALO:HALO + ts, :] = glu(cur_ref[...])
    e_ref[HALO + ts:, :] = glu(next_ref[...]) * keep_next

    off = HALO - CONV_PAD
    rows = es_ref.shape[1]
    for r in range(1, SUBLANES):
        es_ref[r - 1] = e_ref[r:r + rows, :]
    for c in range(cw // LANES):
        cs = slice(c * LANES, (c + 1) * LANES)
        acc = jnp.broadcast_to(b_ref[:, cs], (ts, LANES))
        for r in range(SUBLANES):
            x = e_ref[0:rows, cs] if r == 0 else es_ref[r - 1, :, cs]
            for base in range(0, rows - ts + 1, SUBLANES):
                j = base + r - off
                if 0 <= j < CONV_KERNEL:
                    acc = acc + w_ref[j:j + 1, cs] * x[base:base + ts]
        c_ref[:, cs] = acc

    u = c_ref[...]
    mu = jnp.mean(u, axis=-1, keepdims=True)
    var = jnp.mean(jnp.square(u - mu), axis=-1, keepdims=True)
    y = (u - mu) * lax.rsqrt(var + EPS) * lg_ref[...] + lb_ref[...]
    o_ref[...] = (y * jax.nn.sigmoid(y)).astype(o_ref.dtype)


def _conv_module(zc, conv_w, conv_b, ln_g, ln_b, n_p, s_p, s_s):
    t = zc.shape[0]
    cw = zc.shape[1] // 2
    ts = _tile(math.gcd(s_p, s_s), 256)
    hb = ts // HALO
    last_hb = t // HALO - 1
    shifted_extra = (HALO - CONV_PAD + CONV_KERNEL - 1) // SUBLANES * SUBLANES
    kern = functools.partial(_conv_kernel, ts=ts, cw=cw, n_p=n_p, s_p=s_p, s_s=s_s)
    return pl.pallas_call(
        kern,
        grid=(t // ts,),
        in_specs=[
            pl.BlockSpec((HALO, 2 * cw), lambda i: (jnp.maximum(i * hb - 1, 0), 0)),
            pl.BlockSpec((ts, 2 * cw), lambda i: (i, 0)),
            pl.BlockSpec((HALO, 2 * cw), lambda i: (jnp.minimum((i + 1) * hb, last_hb), 0)),
            pl.BlockSpec((CONV_KERNEL, cw), lambda i: (0, 0)),
            pl.BlockSpec((1, cw), lambda i: (0, 0)),
            pl.BlockSpec((1, cw), lambda i: (0, 0)),
            pl.BlockSpec((1, cw), lambda i: (0, 0)),
        ],
        out_specs=pl.BlockSpec((ts, cw), lambda i: (i, 0)),
        out_shape=jax.ShapeDtypeStruct((t, cw), jnp.bfloat16),
        scratch_shapes=[pltpu.VMEM((ts + 2 * HALO, cw), jnp.float32),
                        pltpu.VMEM((SUBLANES - 1, ts + shifted_extra, cw), jnp.float32),
                        pltpu.VMEM((ts, cw), jnp.float32)],
        compiler_params=_params("parallel"),
        name="conv_module",
    )(zc, zc, zc, conv_w, conv_b.reshape(1, cw), ln_g.reshape(1, cw), ln_b.reshape(1, cw))


def _rel_bucket(rel):
    half = N_BUCKETS // 2
    max_exact = half // 2
    ret = (rel > 0).astype(jnp.int32) * half
    n = jnp.abs(rel)
    nf = jnp.maximum(n, 1).astype(jnp.float32)
    large = max_exact + (jnp.log(nf / max_exact) / math.log(MAX_DISTANCE / max_exact)
                         * (half - max_exact)).astype(jnp.int32)
    large = jnp.minimum(large, half - 1)
    return ret + jnp.where(n < max_exact, n, large)


def _bias_tiles(rel_bias, tq, tk):
    assert tk > MAX_DISTANCE and tq % tk == 0
    d = jnp.arange(-2, tq // tk + 2, dtype=jnp.int32)[:, None, None]
    kk = jnp.arange(tk, dtype=jnp.int32)[None, :, None]
    qq = jnp.arange(tq, dtype=jnp.int32)[None, None, :]
    buckets = _rel_bucket(d * tk + kk - qq)[None]
    table = rel_bias.astype(jnp.float32).T * LOG2E
    out = jnp.zeros((table.shape[0],) + buckets.shape[1:], jnp.float32)
    for bkt in range(N_BUCKETS):
        out = jnp.where(buckets == bkt, table[:, bkt][:, None, None, None], out)
    return out


def _attn_kernel(lam_ref, q_ref, k_ref, vt_ref, bias_ref, g_ref, o_ref, qm_ref, st_ref, sc_ref,
                 *, tq, tk, post_scale):
    i = pl.program_id(1)
    nkv = k_ref.shape[0] // tk
    strips = tq // tk
    q = q_ref[...]
    lane = lax.broadcasted_iota(jnp.int32, q.shape, 1)
    zero = jnp.zeros_like(q)
    qm_ref[0] = jnp.where(lane < HEAD_DIM, q, zero)
    qm_ref[1] = jnp.where(lane >= HEAD_DIM, q, zero)
    r_m, r_l = V_DIM, V_DIM + SUBLANES
    st_ref[:, :r_m, :] = jnp.zeros((2, r_m, tq), jnp.float32)
    st_ref[:, r_m:r_l, :] = jnp.full((2, SUBLANES, tq), -jnp.inf, jnp.float32)
    st_ref[:, r_l:, :] = jnp.zeros((2, SUBLANES, tq), jnp.float32)

    chains = [(mp, slice(s_i * tk, (s_i + 1) * tk)) for mp in range(2) for s_i in range(strips)]

    def scores(c):
        k = k_ref[pl.ds(pl.multiple_of(c * tk, tk), tk), :]
        tile = jnp.clip(c - i * strips, -2, strips + 1) + 2
        return jnp.stack([lax.dot_general(k, qm_ref[mp, qs, :], _NT, preferred_element_type=jnp.float32)
                          + bias_ref[0, tile, :, qs] for mp, qs in chains])

    sc_ref[0] = scores(0)

    def chunk(c, slot):
        nxt = scores(jnp.minimum(c + 1, nkv - 1))
        vt = vt_ref[0, c]
        tile = jnp.clip(c - i * strips, -2, strips + 1) + 2
        st = st_ref[...]
        stats, probs = [], []
        for n, (mp, qs) in enumerate(chains):
            m_prev = st[mp, r_m:r_l, qs]
            m_new = jnp.maximum(m_prev, jnp.max(sc_ref[slot, n], axis=0, keepdims=True))
            stats.append((jnp.exp2(m_prev - m_new), m_new))
            probs.append(jnp.exp2(sc_ref[slot, n] - m_new[:1]).astype(vt.dtype))
        blocks = []
        for (mp, qs), p, (alpha, m_new) in zip(chains, probs, stats):
            pv = jnp.dot(vt, p, preferred_element_type=jnp.float32)
            a_new = alpha[:1] * st[mp, :r_m, qs] + pv[:r_m]
            l_new = alpha * st[mp, r_l:, qs] + pv[r_m:r_l]
            blocks.append(jnp.concatenate([a_new, m_new, l_new], axis=0))
        st_ref[...] = jnp.stack([jnp.concatenate(blocks[mp * strips:(mp + 1) * strips], axis=1)
                                 for mp in range(2)])
        sc_ref[1 - slot] = nxt

    def body(j, _):
        chunk(2 * j, 0)
        chunk(2 * j + 1, 1)
        return 0

    pairs = nkv // 2
    lax.fori_loop(0, pairs, body, 0, unroll=_tile(pairs, min(KV_GROUP // 2, max(pairs // 2, 1))))

    inv1 = 1.0 / st_ref[0, r_l:r_l + 1, :]
    inv2 = lam_ref[0] / st_ref[1, r_l:r_l + 1, :]
    o = st_ref[0, :r_m, :] * inv1 - st_ref[1, :r_m, :] * inv2
    ms = jnp.mean(o * o, axis=0, keepdims=True)
    o = (o * lax.rsqrt(ms + EPS)).T * (g_ref[...] * post_scale)
    o_ref[...] = o.astype(o_ref.dtype)


def _diff_attention(zqkv, vt, row0, batch, seq, n_heads, lam, bias_tiles, subln_g, post_scale):
    _, n_tiles, tk, tq = bias_tiles.shape
    nt = seq // tq
    rb0 = row0 // tq
    kern = functools.partial(_attn_kernel, tq=tq, tk=tk, post_scale=post_scale)
    return pl.pallas_call(
        kern,
        grid=(batch * n_heads, nt),
        in_specs=[
            pl.BlockSpec(memory_space=pltpu.SMEM),
            pl.BlockSpec((tq, V_DIM), lambda bh, i: (rb0 + (bh // n_heads) * nt + i, bh % n_heads)),
            pl.BlockSpec((seq, V_DIM), lambda bh, i: (row0 // seq + bh // n_heads, n_heads + bh % n_heads)),
            pl.BlockSpec((1, seq // tk, vt.shape[2], tk),
                         lambda bh, i: (bh % n_heads, row0 // seq + bh // n_heads, 0, 0)),
            pl.BlockSpec((1, n_tiles, tk, tq), lambda bh, i: (bh % n_heads, 0, 0, 0)),
            pl.BlockSpec((1, V_DIM), lambda bh, i: (0, 0)),
        ],
        out_specs=pl.BlockSpec((tq, V_DIM), lambda bh, i: ((bh // n_heads) * nt + i, bh % n_heads)),
        out_shape=jax.ShapeDtypeStruct((batch * seq, n_heads * V_DIM), jnp.bfloat16),
        scratch_shapes=[pltpu.VMEM((2, tq, V_DIM), jnp.bfloat16),
                        pltpu.VMEM((2, V_DIM + 2 * SUBLANES, tq), jnp.float32),
                        pltpu.VMEM((2, 2 * tq // tk, tk, tk), jnp.float32)],
        compiler_params=_params("parallel", "arbitrary"),
        name="diff_attention",
    )(lam.reshape(1), zqkv, zqkv, vt, bias_tiles, subln_g.reshape(1, V_DIM))


def _outproj_kernel(x_ref, u_ref, a_ref, wu_ref, wa_ref, g_ref, wr_ref, x1_ref, h_ref, aff_ref, *, n_experts):
    y = jnp.dot(u_ref[...], wu_ref[...], preferred_element_type=jnp.float32)
    y = y + jnp.dot(a_ref[...], wa_ref[...], preferred_element_type=jnp.float32)
    x1 = x_ref[...] + y
    x1_ref[...] = x1
    ms = jnp.mean(x1 * x1, axis=-1, keepdims=True)
    h = (x1 * lax.rsqrt(ms + EPS) * g_ref[...]).astype(h_ref.dtype)
    h_ref[...] = h
    logits = jnp.dot(h, wr_ref[...], preferred_element_type=jnp.float32)
    lane = lax.broadcasted_iota(jnp.int32, logits.shape, 1)
    logits = jnp.where(lane < n_experts, logits, -jnp.inf)
    e = jnp.exp(logits - jnp.max(logits, axis=-1, keepdims=True))
    aff_ref[...] = e / jnp.sum(e, axis=-1, keepdims=True)


def _outproj_router(x, u, a, w_u, w_a, g, w_r, n_experts):
    t, d = x.shape
    cw, aw = u.shape[1], a.shape[1]
    tm = _tile(t, 256)
    kern = functools.partial(_outproj_kernel, n_experts=n_experts)
    return pl.pallas_call(
        kern,
        grid=(t // tm,),
        in_specs=[
            pl.BlockSpec((tm, d), lambda i: (i, 0)),
            pl.BlockSpec((tm, cw), lambda i: (i, 0)),
            pl.BlockSpec((tm, aw), lambda i: (i, 0)),
            pl.BlockSpec((cw, d), lambda i: (0, 0)),
            pl.BlockSpec((aw, d), lambda i: (0, 0)),
            pl.BlockSpec((1, d), lambda i: (0, 0)),
            pl.BlockSpec((d, LANES), lambda i: (0, 0)),
        ],
        out_specs=[
            pl.BlockSpec((tm, d), lambda i: (i, 0)),
            pl.BlockSpec((tm, d), lambda i: (i, 0)),
            pl.BlockSpec((tm, LANES), lambda i: (i, 0)),
        ],
        out_shape=[
            jax.ShapeDtypeStruct((t, d), jnp.float32),
            jax.ShapeDtypeStruct((t, d), jnp.bfloat16),
            jax.ShapeDtypeStruct((t, LANES), jnp.float32),
        ],
        compiler_params=_params("parallel"),
        name="outproj_router",
    )(x, u, a, w_u, w_a, g.reshape(1, d), w_r)


def _gate_up_kernel(x_ref, wg_ref, wu_ref, o_ref):
    x = x_ref[0]
    g = jnp.dot(x, wg_ref[0, 0].astype(x.dtype), preferred_element_type=jnp.float32)
    u = jnp.dot(x, wu_ref[0, 0].astype(x.dtype), preferred_element_type=jnp.float32)
    o_ref[0] = (g * jax.nn.sigmoid(g) * u).astype(o_ref.dtype)


def _expert_gate_up(xe, w_gate, w_up, layer, e0):
    e, m, d = xe.shape
    f = w_gate.shape[3]
    tm, tn = _tile(m, 1536), _tile(f, 512)
    return pl.pallas_call(
        _gate_up_kernel,
        grid=(e, f // tn, m // tm),
        in_specs=[
            pl.BlockSpec((1, tm, d), lambda ei, n, mi: (ei, mi, 0)),
            pl.BlockSpec((1, 1, d, tn), lambda ei, n, mi: (layer, e0 + ei, 0, n)),
            pl.BlockSpec((1, 1, d, tn), lambda ei, n, mi: (layer, e0 + ei, 0, n)),
        ],
        out_specs=pl.BlockSpec((1, tm, tn), lambda ei, n, mi: (ei, mi, n)),
        out_shape=jax.ShapeDtypeStruct((e, m, f), jnp.bfloat16),
        compiler_params=_params("parallel", "parallel", "arbitrary"),
        name="expert_gate_up",
    )(xe, w_gate, w_up)


def _down_kernel(h_ref, w_ref, gate_ref, o_ref):
    h = h_ref[0]
    y = jnp.dot(h, w_ref[0, 0].astype(h.dtype), preferred_element_type=jnp.float32)
    reps = y.shape[1] // LANES
    o_ref[0] = y * jnp.concatenate([gate_ref[0]] * reps, axis=1)


def _expert_down(h, w_down, gates_b, layer, e0):
    e, m, f = h.shape
    d = w_down.shape[3]
    tm, tn = _tile(m, 1024), _tile(d, 512)
    return pl.pallas_call(
        _down_kernel,
        grid=(e, m // tm, d // tn),
        in_specs=[
            pl.BlockSpec((1, tm, f), lambda ei, mi, n: (ei, mi, 0)),
            pl.BlockSpec((1, 1, f, tn), lambda ei, mi, n: (layer, e0 + ei, 0, n)),
            pl.BlockSpec((1, tm, LANES), lambda ei, mi, n: (ei, mi, 0)),
        ],
        out_specs=pl.BlockSpec((1, tm, tn), lambda ei, mi, n: (ei, mi, n)),
        out_shape=jax.ShapeDtypeStruct((e, m, d), jnp.float32),
        compiler_params=_params("parallel", "parallel", "arbitrary"),
        name="expert_down",
    )(h, w_down, gates_b)


def _rmsnorm_kernel(x_ref, g_ref, o_ref):
    x = x_ref[...]
    ms = jnp.mean(x * x, axis=-1, keepdims=True)
    o_ref[...] = x * lax.rsqrt(ms + EPS) * g_ref[...]


def _rmsnorm(x, g, row0, rows):
    d = x.shape[1]
    tm = _tile(math.gcd(row0, rows) if row0 else rows, 512)
    rb0 = row0 // tm
    return pl.pallas_call(
        _rmsnorm_kernel,
        grid=(rows // tm,),
        in_specs=[pl.BlockSpec((tm, d), lambda i: (rb0 + i, 0)), pl.BlockSpec((1, d), lambda i: (0, 0))],
        out_specs=pl.BlockSpec((tm, d), lambda i: (i, 0)),
        out_shape=jax.ShapeDtypeStruct((rows, d), jnp.float32),
        compiler_params=_params("parallel"),
        name="final_rmsnorm",
    )(x, g.reshape(1, d))


def _route(aff, n_experts):
    n = aff.shape[0]
    cap = CAPACITY_FACTOR * n // n_experts
    gates, idx = lax.top_k(aff[:, :n_experts].T, cap)
    idx, gates = lax.sort((idx, gates), dimension=1, num_keys=1)
    return gates, idx


def kernel(x_prompt, x_sample, rel_bias, final_g, ln1_g, w_in, conv_w, conv_b, conv_ln_g, conv_ln_b,
           lam_q1, lam_k1, lam_q2, lam_k2, subln_g, w_out, ln2_g, w_router, w_gate, w_up, w_down):
    bp, sp, d = x_prompt.shape
    bs, ss, _ = x_sample.shape
    n_p, n_s = bp * sp, bs * ss
    depth = w_in.shape[0]
    cw = conv_w.shape[2]
    n_heads = (w_in.shape[2] - 2 * cw) // (3 * V_DIM)
    n_experts = w_router.shape[2]
    bf16 = jnp.bfloat16

    x = jnp.concatenate([x_prompt.reshape(n_p, d), x_sample.reshape(n_s, d)], axis=0)
    tq_attn = _tile(math.gcd(sp, ss), 512)
    tk_attn = _tile(tq_attn, 256)
    bias_tiles = _bias_tiles(rel_bias, tq_attn, tk_attn)
    qk_w = n_heads * V_DIM
    qkv_scale = jnp.concatenate([jnp.full((qk_w,), LOG2E * HEAD_DIM ** -0.5, jnp.float32),
                                 jnp.ones((2 * qk_w,), jnp.float32)])

    for l in range(depth):
        w_in_l = w_in[l].astype(bf16)
        zc = _norm_matmul(x, ln1_g[l], w_in_l[:, :2 * cw], jnp.ones((2 * cw,), jnp.float32), jnp.float32)
        zqkv = _norm_matmul(x, ln1_g[l], w_in_l[:, 2 * cw:], qkv_scale, bf16)

        u = _conv_module(zc, conv_w[l], conv_b[l], conv_ln_g[l], conv_ln_b[l], n_p, sp, ss)

        lam_init = 0.8 - 0.6 * math.exp(-0.3 * l)
        lam = (jnp.exp(jnp.sum(lam_q1[l] * lam_k1[l])) - jnp.exp(jnp.sum(lam_q2[l] * lam_k2[l])) + lam_init)
        vt = zqkv[:, 2 * qk_w:].reshape((n_p + n_s) // tk_attn, tk_attn, n_heads, V_DIM).transpose(2, 0, 3, 1)
        vt = jnp.concatenate([vt, jnp.ones(vt.shape[:2] + (ONES_ROWS, tk_attn), bf16)], axis=2)
        attn = functools.partial(_diff_attention, zqkv, vt, n_heads=n_heads, lam=lam, bias_tiles=bias_tiles,
                                 subln_g=subln_g[l], post_scale=1.0 - lam_init)
        a = jnp.concatenate([attn(row0=0, batch=bp, seq=sp), attn(row0=n_p, batch=bs, seq=ss)], axis=0)

        w_out_l = w_out[l].astype(bf16)
        w_r = jnp.pad(w_router[l], ((0, 0), (0, LANES - n_experts))).astype(bf16)
        x1, h, aff = _outproj_router(x, u, a, w_out_l[:cw], w_out_l[cw:], ln2_g[l], w_r, n_experts)

        gates_p, idx_p = _route(aff[:n_p], n_experts)
        gates_s, idx_s = _route(aff[n_p:], n_experts)
        gates = jnp.concatenate([gates_p, gates_s], axis=1)
        idx = jnp.concatenate([idx_p, idx_s + n_p], axis=1)

        x = x1
        eg = _tile(n_experts, EXPERT_GROUP)
        for e0 in range(0, n_experts, eg):
            idx_g = idx[e0:e0 + eg]
            hid = _expert_gate_up(h[idx_g], w_gate, w_up, l, e0)
            gates_b = jnp.broadcast_to(gates[e0:e0 + eg, :, None], idx_g.shape + (LANES,))
            ye = _expert_down(hid, w_down, gates_b, l, e0)
            x = x.at[idx_g.reshape(-1)].add(ye.reshape(-1, d))

    y_p = _rmsnorm(x, final_g, 0, n_p)
    y_s = _rmsnorm(x, final_g, n_p, n_s)
    return y_p.reshape(bp, sp, d), y_s.reshape(bs, ss, d)
```

```python
import functools
import math

import jax
import jax.numpy as jnp
from jax import lax
from jax.experimental import pallas as pl
from jax.experimental.pallas import tpu as pltpu

EPS = 1e-6
CONV_KERNEL = 31
CONV_PAD = CONV_KERNEL // 2
HALO = 16
HEAD_DIM = 64
V_DIM = 2 * HEAD_DIM
N_BUCKETS = 32
MAX_DISTANCE = 128
CAPACITY_FACTOR = 2
LANES = 128
SUBLANES = 8
EXPERT_GROUP = 16
KV_GROUP = 16
ONES_ROWS = 16
LOG2E = math.log2(math.e)
VMEM_LIMIT_BYTES = 56 * 1024 * 1024

_NT = (((1,), (1,)), ((), ()))


def _tile(n, pref):
    t = min(n, pref)
    while n % t:
        t -= 1
    return t


def _params(*sem):
    return pltpu.CompilerParams(dimension_semantics=sem, vmem_limit_bytes=VMEM_LIMIT_BYTES)


def _norm_matmul_kernel(x_ref, g_ref, w_ref, s_ref, o_ref, h_ref):
    @pl.when(pl.program_id(1) == 0)
    def _():
        x = x_ref[...]
        ms = jnp.mean(x * x, axis=-1, keepdims=True)
        h_ref[...] = (x * lax.rsqrt(ms + EPS) * g_ref[...]).astype(h_ref.dtype)

    y = jnp.dot(h_ref[...], w_ref[...], preferred_element_type=jnp.float32)
    o_ref[...] = (y * s_ref[...]).astype(o_ref.dtype)


def _norm_matmul(x, g, w, col_scale, out_dtype):
    t, d = x.shape
    n = w.shape[1]
    tm, tn = _tile(t, 1024), _tile(n, 1024)
    return pl.pallas_call(
        _norm_matmul_kernel,
        grid=(t // tm, n // tn),
        in_specs=[
            pl.BlockSpec((tm, d), lambda i, j: (i, 0)),
            pl.BlockSpec((1, d), lambda i, j: (0, 0)),
            pl.BlockSpec((d, tn), lambda i, j: (0, j)),
            pl.BlockSpec((1, tn), lambda i, j: (0, j)),
        ],
        out_specs=pl.BlockSpec((tm, tn), lambda i, j: (i, j)),
        out_shape=jax.ShapeDtypeStruct((t, n), out_dtype),
        scratch_shapes=[pltpu.VMEM((tm, d), jnp.bfloat16)],
        compiler_params=_params("parallel", "arbitrary"),
        name="norm_matmul",
    )(x, g.reshape(1, d), w, col_scale.reshape(1, n))


def _conv_kernel(prev_ref, cur_ref, next_ref, w_ref, b_ref, lg_ref, lb_ref, o_ref, e_ref, es_ref, c_ref,
                 *, ts, cw, n_p, s_p, s_s):
    r0 = pl.program_id(0) * ts
    in_p = r0 < n_p
    local = jnp.where(in_p, r0 % s_p, (r0 - n_p) % s_s)
    slen = jnp.where(in_p, s_p, s_s)
    keep_prev = (local != 0).astype(jnp.float32)
    keep_next = (local + ts != slen).astype(jnp.float32)

    def glu(z):
        return z[:, :cw] * jax.nn.sigmoid(z[:, cw:])

    e_ref[0:HALO, :] = glu(prev_ref[...]) * keep_prev
    e_ref[HALO:HALO + ts, :] = glu(cur_ref[...])
    e_ref[HALO + ts:, :] = glu(next_ref[...]) * keep_next

    off = HALO - CONV_PAD
    rows = es_ref.shape[1]
    for r in range(1, SUBLANES):
        es_ref[r - 1] = e_ref[r:r + rows, :]
    for c in range(cw // LANES):
        cs = slice(c * LANES, (c + 1) * LANES)
        acc = jnp.broadcast_to(b_ref[:, cs], (ts, LANES))
        for r in range(SUBLANES):
            x = e_ref[0:rows, cs] if r == 0 else es_ref[r - 1, :, cs]
            for base in range(0, rows - ts + 1, SUBLANES):
                j = base + r - off
                if 0 <= j < CONV_KERNEL:
                    acc = acc + w_ref[j:j + 1, cs] * x[base:base + ts]
        c_ref[:, cs] = acc

    u = c_ref[...]
    mu = jnp.mean(u, axis=-1, keepdims=True)
    var = jnp.mean(jnp.square(u - mu), axis=-1, keepdims=True)
    y = (u - mu) * lax.rsqrt(var + EPS) * lg_ref[...] + lb_ref[...]
    o_ref[...] = (y * jax.nn.sigmoid(y)).astype(o_ref.dtype)


def _conv_module(zc, conv_w, conv_b, ln_g, ln_b, n_p, s_p, s_s):
    t = zc.shape[0]
    cw = zc.shape[1] // 2
    ts = _tile(math.gcd(s_p, s_s), 256)
    hb = ts // HALO
    last_hb = t // HALO - 1
    shifted_extra = (HALO - CONV_PAD + CONV_KERNEL - 1) // SUBLANES * SUBLANES
    kern = functools.partial(_conv_kernel, ts=ts, cw=cw, n_p=n_p, s_p=s_p, s_s=s_s)
    return pl.pallas_call(
        kern,
        grid=(t // ts,),
        in_specs=[
            pl.BlockSpec((HALO, 2 * cw), lambda i: (jnp.maximum(i * hb - 1, 0), 0)),
            pl.BlockSpec((ts, 2 * cw), lambda i: (i, 0)),
            pl.BlockSpec((HALO, 2 * cw), lambda i: (jnp.minimum((i + 1) * hb, last_hb), 0)),
            pl.BlockSpec((CONV_KERNEL, cw), lambda i: (0, 0)),
            pl.BlockSpec((1, cw), lambda i: (0, 0)),
            pl.BlockSpec((1, cw), lambda i: (0, 0)),
            pl.BlockSpec((1, cw), lambda i: (0, 0)),
        ],
        out_specs=pl.BlockSpec((ts, cw), lambda i: (i, 0)),
        out_shape=jax.ShapeDtypeStruct((t, cw), jnp.bfloat16),
        scratch_shapes=[pltpu.VMEM((ts + 2 * HALO, cw), jnp.float32),
                        pltpu.VMEM((SUBLANES - 1, ts + shifted_extra, cw), jnp.float32),
                        pltpu.VMEM((ts, cw), jnp.float32)],
        compiler_params=_params("parallel"),
        name="conv_module",
    )(zc, zc, zc, conv_w, conv_b.reshape(1, cw), ln_g.reshape(1, cw), ln_b.reshape(1, cw))


def _rel_bucket(rel):
    half = N_BUCKETS // 2
    max_exact = half // 2
    ret = (rel > 0).astype(jnp.int32) * half
    n = jnp.abs(rel)
    nf = jnp.maximum(n, 1).astype(jnp.float32)
    large = max_exact + (jnp.log(nf / max_exact) / math.log(MAX_DISTANCE / max_exact)
                         * (half - max_exact)).astype(jnp.int32)
    large = jnp.minimum(large, half - 1)
    return ret + jnp.where(n < max_exact, n, large)


def _bias_tiles(rel_bias, tq, tk):
    assert tk > MAX_DISTANCE and tq % tk == 0
    d = jnp.arange(-2, tq // tk + 2, dtype=jnp.int32)[:, None, None]
    kk = jnp.arange(tk, dtype=jnp.int32)[None, :, None]
    qq = jnp.arange(tq, dtype=jnp.int32)[None, None, :]
    buckets = _rel_bucket(d * tk + kk - qq)[None]
    table = rel_bias.astype(jnp.float32).T * LOG2E
    out = jnp.zeros((table.shape[0],) + buckets.shape[1:], jnp.float32)
    for bkt in range(N_BUCKETS):
        out = jnp.where(buckets == bkt, table[:, bkt][:, None, None, None], out)
    return out


def _attn_kernel(lam_ref, q_ref, k_ref, vt_ref, bias_ref, g_ref, o_ref, qm_ref, st_ref, sc_ref,
                 *, tq, tk, post_scale):
    i = pl.program_id(1)
    nkv = k_ref.shape[0] // tk
    strips = tq // tk
    q = q_ref[...]
    lane = lax.broadcasted_iota(jnp.int32, q.shape, 1)
    zero = jnp.zeros_like(q)
    qm_ref[0] = jnp.where(lane < HEAD_DIM, q, zero)
    qm_ref[1] = jnp.where(lane >= HEAD_DIM, q, zero)
    r_m, r_l = V_DIM, V_DIM + SUBLANES
    st_ref[:, :r_m, :] = jnp.zeros((2, r_m, tq), jnp.float32)
    st_ref[:, r_m:r_l, :] = jnp.full((2, SUBLANES, tq), -jnp.inf, jnp.float32)
    st_ref[:, r_l:, :] = jnp.zeros((2, SUBLANES, tq), jnp.float32)

    chains = [(mp, slice(s_i * tk, (s_i + 1) * tk)) for mp in range(2) for s_i in range(strips)]

    def scores(c):
        k = k_ref[pl.ds(pl.multiple_of(c * tk, tk), tk), :]
        tile = jnp.clip(c - i * strips, -2, strips + 1) + 2
        return jnp.stack([lax.dot_general(k, qm_ref[mp, qs, :], _NT, preferred_element_type=jnp.float32)
                          + bias_ref[0, tile, :, qs] for mp, qs in chains])

    sc_ref[0] = scores(0)

    def chunk(c, slot):
        nxt = scores(jnp.minimum(c + 1, nkv - 1))
        vt = vt_ref[0, c]
        tile = jnp.clip(c - i * strips, -2, strips + 1) + 2
        st = st_ref[...]
        stats, probs = [], []
        for n, (mp, qs) in enumerate(chains):
            m_prev = st[mp, r_m:r_l, qs]
            m_new = jnp.maximum(m_prev, jnp.max(sc_ref[slot, n], axis=0, keepdims=True))
            stats.append((jnp.exp2(m_prev - m_new), m_new))
            probs.append(jnp.exp2(sc_ref[slot, n] - m_new[:1]).astype(vt.dtype))
        blocks = []
        for (mp, qs), p, (alpha, m_new) in zip(chains, probs, stats):
            pv = jnp.dot(vt, p, preferred_element_type=jnp.float32)
            a_new = alpha[:1] * st[mp, :r_m, qs] + pv[:r_m]
            l_new = alpha * st[mp, r_l:, qs] + pv[r_m:r_l]
            blocks.append(jnp.concatenate([a_new, m_new, l_new], axis=0))
        st_ref[...] = jnp.stack([jnp.concatenate(blocks[mp * strips:(mp + 1) * strips], axis=1)
                                 for mp in range(2)])
        sc_ref[1 - slot] = nxt

    def body(j, _):
        chunk(2 * j, 0)
        chunk(2 * j + 1, 1)
        return 0

    pairs = nkv // 2
    lax.fori_loop(0, pairs, body, 0, unroll=_tile(pairs, min(KV_GROUP // 2, max(pairs // 2, 1))))

    inv1 = 1.0 / st_ref[0, r_l:r_l + 1, :]
    inv2 = lam_ref[0] / st_ref[1, r_l:r_l + 1, :]
    o = st_ref[0, :r_m, :] * inv1 - st_ref[1, :r_m, :] * inv2
    ms = jnp.mean(o * o, axis=0, keepdims=True)
    o = (o * lax.rsqrt(ms + EPS)).T * (g_ref[...] * post_scale)
    o_ref[...] = o.astype(o_ref.dtype)


def _diff_attention(zqkv, vt, row0, batch, seq, n_heads, lam, bias_tiles, subln_g, post_scale):
    _, n_tiles, tk, tq = bias_tiles.shape
    nt = seq // tq
    rb0 = row0 // tq
    kern = functools.partial(_attn_kernel, tq=tq, tk=tk, post_scale=post_scale)
    return pl.pallas_call(
        kern,
        grid=(batch * n_heads, nt),
        in_specs=[
            pl.BlockSpec(memory_space=pltpu.SMEM),
            pl.BlockSpec((tq, V_DIM), lambda bh, i: (rb0 + (bh // n_heads) * nt + i, bh % n_heads)),
            pl.BlockSpec((seq, V_DIM), lambda bh, i: (row0 // seq + bh // n_heads, n_heads + bh % n_heads)),
            pl.BlockSpec((1, seq // tk, vt.shape[2], tk),
                         lambda bh, i: (bh % n_heads, row0 // seq + bh // n_heads, 0, 0)),
            pl.BlockSpec((1, n_tiles, tk, tq), lambda bh, i: (bh % n_heads, 0, 0, 0)),
            pl.BlockSpec((1, V_DIM), lambda bh, i: (0, 0)),
        ],
        out_specs=pl.BlockSpec((tq, V_DIM), lambda bh, i: ((bh // n_heads) * nt + i, bh % n_heads)),
        out_shape=jax.ShapeDtypeStruct((batch * seq, n_heads * V_DIM), jnp.bfloat16),
        scratch_shapes=[pltpu.VMEM((2, tq, V_DIM), jnp.bfloat16),
                        pltpu.VMEM((2, V_DIM + 2 * SUBLANES, tq), jnp.float32),
                        pltpu.VMEM((2, 2 * tq // tk, tk, tk), jnp.float32)],
        compiler_params=_params("parallel", "arbitrary"),
        name="diff_attention",
    )(lam.reshape(1), zqkv, zqkv, vt, bias_tiles, subln_g.reshape(1, V_DIM))


def _outproj_kernel(x_ref, u_ref, a_ref, wu_ref, wa_ref, g_ref, wr_ref, x1_ref, h_ref, aff_ref, *, n_experts):
    y = jnp.dot(u_ref[...], wu_ref[...], preferred_element_type=jnp.float32)
    y = y + jnp.dot(a_ref[...], wa_ref[...], preferred_element_type=jnp.float32)
    x1 = x_ref[...] + y
    x1_ref[...] = x1
    ms = jnp.mean(x1 * x1, axis=-1, keepdims=True)
    h = (x1 * lax.rsqrt(ms + EPS) * g_ref[...]).astype(h_ref.dtype)
    h_ref[...] = h
    logits = jnp.dot(h, wr_ref[...], preferred_element_type=jnp.float32)
    lane = lax.broadcasted_iota(jnp.int32, logits.shape, 1)
    logits = jnp.where(lane < n_experts, logits, -jnp.inf)
    e = jnp.exp(logits - jnp.max(logits, axis=-1, keepdims=True))
    aff_ref[...] = e / jnp.sum(e, axis=-1, keepdims=True)


def _outproj_router(x, u, a, w_u, w_a, g, w_r, n_experts):
    t, d = x.shape
    cw, aw = u.shape[1], a.shape[1]
    tm = _tile(t, 256)
    kern = functools.partial(_outproj_kernel, n_experts=n_experts)
    return pl.pallas_call(
        kern,
        grid=(t // tm,),
        in_specs=[
            pl.BlockSpec((tm, d), lambda i: (i, 0)),
            pl.BlockSpec((tm, cw), lambda i: (i, 0)),
            pl.BlockSpec((tm, aw), lambda i: (i, 0)),
            pl.BlockSpec((cw, d), lambda i: (0, 0)),
            pl.BlockSpec((aw, d), lambda i: (0, 0)),
            pl.BlockSpec((1, d), lambda i: (0, 0)),
            pl.BlockSpec((d, LANES), lambda i: (0, 0)),
        ],
        out_specs=[
            pl.BlockSpec((tm, d), lambda i: (i, 0)),
            pl.BlockSpec((tm, d), lambda i: (i, 0)),
            pl.BlockSpec((tm, LANES), lambda i: (i, 0)),
        ],
        out_shape=[
            jax.ShapeDtypeStruct((t, d), jnp.float32),
            jax.ShapeDtypeStruct((t, d), jnp.bfloat16),
            jax.ShapeDtypeStruct((t, LANES), jnp.float32),
        ],
        compiler_params=_params("parallel"),
        name="outproj_router",
    )(x, u, a, w_u, w_a, g.reshape(1, d), w_r)


def _gate_up_kernel(x_ref, wg_ref, wu_ref, o_ref):
    x = x_ref[0]
    g = jnp.dot(x, wg_ref[0, 0].astype(x.dtype), preferred_element_type=jnp.float32)
    u = jnp.dot(x, wu_ref[0, 0].astype(x.dtype), preferred_element_type=jnp.float32)
    o_ref[0] = (g * jax.nn.sigmoid(g) * u).astype(o_ref.dtype)


def _expert_gate_up(xe, w_gate, w_up, layer, e0):
    e, m, d = xe.shape
    f = w_gate.shape[3]
    tm, tn = _tile(m, 3072), _tile(f, 256)
    return pl.pallas_call(
        _gate_up_kernel,
        grid=(e, f // tn, m // tm),
        in_specs=[
            pl.BlockSpec((1, tm, d), lambda ei, n, mi: (ei, mi, 0)),
            pl.BlockSpec((1, 1, d, tn), lambda ei, n, mi: (layer, e0 + ei, 0, n)),
            pl.BlockSpec((1, 1, d, tn), lambda ei, n, mi: (layer, e0 + ei, 0, n)),
        ],
        out_specs=pl.BlockSpec((1, tm, tn), lambda ei, n, mi: (ei, mi, n)),
        out_shape=jax.ShapeDtypeStruct((e, m, f), jnp.bfloat16),
        compiler_params=_params("parallel", "parallel", "arbitrary"),
        name="expert_gate_up",
    )(xe, w_gate, w_up)


def _down_kernel(h_ref, w_ref, gate_ref, o_ref):
    h = h_ref[0]
    y = jnp.dot(h, w_ref[0, 0].astype(h.dtype), preferred_element_type=jnp.float32)
    reps = y.shape[1] // LANES
    o_ref[0] = y * jnp.concatenate([gate_ref[0]] * reps, axis=1)


def _expert_down(h, w_down, gates_b, layer, e0):
    e, m, f = h.shape
    d = w_down.shape[3]
    tm, tn = _tile(m, 1024), _tile(d, 512)
    return pl.pallas_call(
        _down_kernel,
        grid=(e, m // tm, d // tn),
        in_specs=[
            pl.BlockSpec((1, tm, f), lambda ei, mi, n: (ei, mi, 0)),
            pl.BlockSpec((1, 1, f, tn), lambda ei, mi, n: (layer, e0 + ei, 0, n)),
            pl.BlockSpec((1, tm, LANES), lambda ei, mi, n: (ei, mi, 0)),
        ],
        out_specs=pl.BlockSpec((1, tm, tn), lambda ei, mi, n: (ei, mi, n)),
        out_shape=jax.ShapeDtypeStruct((e, m, d), jnp.float32),
        compiler_params=_params("parallel", "parallel", "arbitrary"),
        name="expert_down",
    )(h, w_down, gates_b)


def _rmsnorm_kernel(x_ref, g_ref, o_ref):
    x = x_ref[...]
    ms = jnp.mean(x * x, axis=-1, keepdims=True)
    o_ref[...] = x * lax.rsqrt(ms + EPS) * g_ref[...]


def _rmsnorm(x, g, row0, rows):
    d = x.shape[1]
    tm = _tile(math.gcd(row0, rows) if row0 else rows, 512)
    rb0 = row0 // tm
    return pl.pallas_call(
        _rmsnorm_kernel,
        grid=(rows // tm,),
        in_specs=[pl.BlockSpec((tm, d), lambda i: (rb0 + i, 0)), pl.BlockSpec((1, d), lambda i: (0, 0))],
        out_specs=pl.BlockSpec((tm, d), lambda i: (i, 0)),
        out_shape=jax.ShapeDtypeStruct((rows, d), jnp.float32),
        compiler_params=_params("parallel"),
        name="final_rmsnorm",
    )(x, g.reshape(1, d))


def _route(aff, n_experts):
    n = aff.shape[0]
    cap = CAPACITY_FACTOR * n // n_experts
    gates, idx = lax.top_k(aff[:, :n_experts].T, cap)
    idx, gates = lax.sort((idx, gates), dimension=1, num_keys=1)
    return gates, idx


def kernel(x_prompt, x_sample, rel_bias, final_g, ln1_g, w_in, conv_w, conv_b, conv_ln_g, conv_ln_b,
           lam_q1, lam_k1, lam_q2, lam_k2, subln_g, w_out, ln2_g, w_router, w_gate, w_up, w_down):
    bp, sp, d = x_prompt.shape
    bs, ss, _ = x_sample.shape
    n_p, n_s = bp * sp, bs * ss
    depth = w_in.shape[0]
    cw = conv_w.shape[2]
    n_heads = (w_in.shape[2] - 2 * cw) // (3 * V_DIM)
    n_experts = w_router.shape[2]
    bf16 = jnp.bfloat16

    x = jnp.concatenate([x_prompt.reshape(n_p, d), x_sample.reshape(n_s, d)], axis=0)
    tq_attn = _tile(math.gcd(sp, ss), 512)
    tk_attn = _tile(tq_attn, 256)
    bias_tiles = _bias_tiles(rel_bias, tq_attn, tk_attn)
    qk_w = n_heads * V_DIM
    qkv_scale = jnp.concatenate([jnp.full((qk_w,), LOG2E * HEAD_DIM ** -0.5, jnp.float32),
                                 jnp.ones((2 * qk_w,), jnp.float32)])

    for l in range(depth):
        w_in_l = w_in[l].astype(bf16)
        zc = _norm_matmul(x, ln1_g[l], w_in_l[:, :2 * cw], jnp.ones((2 * cw,), jnp.float32), jnp.float32)
        zqkv = _norm_matmul(x, ln1_g[l], w_in_l[:, 2 * cw:], qkv_scale, bf16)

        u = _conv_module(zc, conv_w[l], conv_b[l], conv_ln_g[l], conv_ln_b[l], n_p, sp, ss)

        lam_init = 0.8 - 0.6 * math.exp(-0.3 * l)
        lam = (jnp.exp(jnp.sum(lam_q1[l] * lam_k1[l])) - jnp.exp(jnp.sum(lam_q2[l] * lam_k2[l])) + lam_init)
        vt = zqkv[:, 2 * qk_w:].reshape((n_p + n_s) // tk_attn, tk_attn, n_heads, V_DIM).transpose(2, 0, 3, 1)
        vt = jnp.concatenate([vt, jnp.ones(vt.shape[:2] + (ONES_ROWS, tk_attn), bf16)], axis=2)
        attn = functools.partial(_diff_attention, zqkv, vt, n_heads=n_heads, lam=lam, bias_tiles=bias_tiles,
                                 subln_g=subln_g[l], post_scale=1.0 - lam_init)
        a = jnp.concatenate([attn(row0=0, batch=bp, seq=sp), attn(row0=n_p, batch=bs, seq=ss)], axis=0)

        w_out_l = w_out[l].astype(bf16)
        w_r = jnp.pad(w_router[l], ((0, 0), (0, LANES - n_experts))).astype(bf16)
        x1, h, aff = _outproj_router(x, u, a, w_out_l[:cw], w_out_l[cw:], ln2_g[l], w_r, n_experts)

        gates_p, idx_p = _route(aff[:n_p], n_experts)
        gates_s, idx_s = _route(aff[n_p:], n_experts)
        gates = jnp.concatenate([gates_p, gates_s], axis=1)
        idx = jnp.concatenate([idx_p, idx_s + n_p], axis=1)

        x = x1
        eg = _tile(n_experts, EXPERT_GROUP)
        for e0 in range(0, n_experts, eg):
            idx_g = idx[e0:e0 + eg]
            hid = _expert_gate_up(h[idx_g], w_gate, w_up, l, e0)
            gates_b = jnp.broadcast_to(gates[e0:e0 + eg, :, None], idx_g.shape + (LANES,))
            ye = _expert_down(hid, w_down, gates_b, l, e0)
            x = x.at[idx_g.reshape(-1)].add(ye.reshape(-1, d))

    y_p = _rmsnorm(x, final_g, 0, n_p)
    y_s = _rmsnorm(x, final_g, n_p, n_s)
    return y_p.reshape(bp, sp, d), y_s.reshape(bs, ss, d)
```
